```python
import jax, jax.numpy as jnp
from jax import lax
import numpy as np

D_MODEL = 2048
BATCH = 1
SEQ = 8192
DEPTH = 1
DEC_BATCH = 128
DEC_SEQ = 1
PAST_LEN = 16384
PAGE_SIZE = 128

MEM_TOKENS = 256
MLA_HEADS = 8
Q_RANK = 512
KV_RANK = 512
NOPE_DIM = 128
ROPE_DIM = 64
V_DIM = 128
ROPE_THETA = 10000.0
SB_HEADS = 8
SB_KV_HEADS = 2
SB_GROUP = SB_HEADS // SB_KV_HEADS
SB_DIM = 128
MIX_WIDTH = MLA_HEADS * V_DIM + SB_HEADS * SB_DIM
IN_WIDTH = Q_RANK + KV_RANK + ROPE_DIM + (SB_HEADS + 2 * SB_KV_HEADS) * SB_DIM
MEM_HEADS = 4
MEM_DIM = 128
N_EXPERTS = 32
TOP_K = 4
D_FF = 2048
SWIGLU_LIMIT = 7.0
SWIGLU_ALPHA = 1.702
MOE_BLOCK = 128
Q_BLOCK = 128
EPS = 1e-6
NEG_INF = -1e30
MLA_SCALE = (NOPE_DIM + ROPE_DIM) ** -0.5
SB_SCALE = SB_DIM ** -0.5
MEM_SCALE = MEM_DIM ** -0.5

kernel_name = 'hymba_mla_stickbreak_moe_step'


def rmsnorm(x, g):
    xf = x.astype(jnp.float32)
    y = xf * lax.rsqrt(jnp.mean(xf * xf, axis=-1, keepdims=True) + EPS)
    return (y * g.astype(jnp.float32)).astype(x.dtype)


def rope(x, pos):
    half = x.shape[-1] // 2
    inv = ROPE_THETA ** (-jnp.arange(half, dtype=jnp.float32) / half)
    ang = pos.astype(jnp.float32)[:, None] * inv[None, :]
    cos = jnp.cos(ang)[:, None, :]
    sin = jnp.sin(ang)[:, None, :]
    xf = x.astype(jnp.float32)
    x1, x2 = xf[..., :half], xf[..., half:]
    return jnp.concatenate([x1 * cos - x2 * sin, x2 * cos + x1 * sin], axis=-1).astype(x.dtype)


def project_mixers(h, pos, w_in, q_norm, w_uq, kv_norm):
    B, S, _ = h.shape
    o1 = Q_RANK
    o2 = o1 + KV_RANK
    o3 = o2 + ROPE_DIM
    o4 = o3 + SB_HEADS * SB_DIM
    o5 = o4 + SB_KV_HEADS * SB_DIM
    c_q, c_kv, k_pe, sb_q, sb_k, sb_v = jnp.split(h @ w_in, [o1, o2, o3, o4, o5], axis=-1)
    q = (rmsnorm(c_q, q_norm) @ w_uq).reshape(B, S, MLA_HEADS, NOPE_DIM + ROPE_DIM)
    q_nope = q[..., :NOPE_DIM]
    q_pe = rope(q[..., NOPE_DIM:], pos)
    latent = rmsnorm(c_kv, kv_norm)
    k_pe = rope(k_pe[:, :, None, :], pos)[:, :, 0, :]
    sb_q = sb_q.reshape(B, S, SB_KV_HEADS, SB_GROUP, SB_DIM)
    sb_k = sb_k.reshape(B, S, SB_KV_HEADS, SB_DIM)
    sb_v = sb_v.reshape(B, S, SB_KV_HEADS, SB_DIM)
    return q_nope, q_pe, latent, k_pe, sb_q, sb_k, sb_v


def mla_prompt(q_nope, q_pe, latent, k_pe, w_uk, w_uv):
    B, S = q_nope.shape[:2]
    k_nope = jnp.einsum('bsk,khn->bshn', latent, w_uk)
    v = jnp.einsum('bsk,khv->bshv', latent, w_uv)
    kpos = jnp.arange(S)

    def block(i):
        qs = i * Q_BLOCK
        qn = lax.dynamic_slice_in_dim(q_nope, qs, Q_BLOCK, axis=1)
        qp = lax.dynamic_slice_in_dim(q_pe, qs, Q_BLOCK, axis=1)
        s = (jnp.einsum('bqhn,bkhn->bhqk', qn, k_nope)
             + jnp.einsum('bqhr,bkr->bhqk', qp, k_pe)).astype(jnp.float32) * MLA_SCALE
        causal = kpos[None, :] <= (qs + jnp.arange(Q_BLOCK))[:, None]
        p = jax.nn.softmax(jnp.where(causal, s, NEG_INF), axis=-1)
        return jnp.einsum('bhqk,bkhv->bqhv', p.astype(v.dtype), v)

    out = lax.map(block, jnp.arange(S // Q_BLOCK))
    return jnp.moveaxis(out, 0, 1).reshape(B, S, MLA_HEADS * V_DIM)


def mla_decode(q_nope, q_pe, latent, k_pe, cache_lat, cache_kpe, layer, page_table, w_uk, w_uv):
    B, T = q_nope.shape[:2]
    q_lat = jnp.einsum('bthn,khn->bthk', q_nope, w_uk)

    def scores(lat, kpe):
        return (jnp.einsum('bthk,bsk->bhts', q_lat, lat)
                + jnp.einsum('bthr,bsr->bhts', q_pe, kpe)).astype(jnp.float32) * MLA_SCALE

    causal = jnp.arange(T)[None, :] <= jnp.arange(T)[:, None]
    s = jnp.where(causal, scores(latent, k_pe), NEG_INF)
    m = s.max(-1)
    p = jnp.exp(s - m[..., None])
    l = p.sum(-1)
    acc = jnp.einsum('bhts,bsk->bhtk', p, latent.astype(jnp.float32))

    def step(carry, phys):
        m, l, acc = carry
        lat = cache_lat[layer, phys]
        kpe = cache_kpe[layer, phys]
        s = scores(lat, kpe)
        m_new = jnp.maximum(m, s.max(-1))
        corr = jnp.exp(m - m_new)
        p = jnp.exp(s - m_new[..., None])
        l = l * corr + p.sum(-1)
        acc = acc * corr[..., None] + jnp.einsum('bhts,bsk->bhtk', p, lat.astype(jnp.float32))
        return (m_new, l, acc), None

    (m, l, acc), _ = lax.scan(step, (m, l, acc), page_table.T)
    o_lat = (acc / l[..., None]).astype(q_nope.dtype)
    return jnp.einsum('bhtk,khv->bthv', o_lat, w_uv).reshape(B, T, MLA_HEADS * V_DIM)


def sb_weights(z, valid, later):
    log_1m = jnp.where(valid, jax.nn.log_sigmoid(-z), 0.0)
    suffix = later[..., None] + lax.cumsum(log_1m, axis=z.ndim - 1, reverse=True) - log_1m
    w = jnp.where(valid, jnp.exp(jax.nn.log_sigmoid(z) + suffix), 0.0)
    return w, later + log_1m.sum(-1)


def sb_prompt(q, k, v):
    B, S = q.shape[:2]
    kpos = jnp.arange(S)

    def block(i):
        qs = i * Q_BLOCK
        qb = lax.dynamic_slice_in_dim(q, qs, Q_BLOCK, axis=1)
        z = jnp.einsum('bqcgd,bkcd->bcgqk', qb, k).astype(jnp.float32) * SB_SCALE
        valid = kpos[None, :] < (qs + jnp.arange(Q_BLOCK))[:, None]
        w, _ = sb_weights(z, valid, jnp.zeros(z.shape[:-1], jnp.float32))
        return jnp.einsum('bcgqk,bkcd->bqcgd', w.astype(v.dtype), v)

    out = lax.map(block, jnp.arange(S // Q_BLOCK))
    return jnp.moveaxis(out, 0, 1).reshape(B, S, SB_HEADS * SB_DIM)


def sb_decode(q, k_new, v_new, cache_k, cache_v, layer, page_table):
    B, T = q.shape[:2]
    z = jnp.einsum('btcgd,bscd->bcgts', q, k_new).astype(jnp.float32) * SB_SCALE
    causal = jnp.arange(T)[None, :] < jnp.arange(T)[:, None]
    w, later = sb_weights(z, causal, jnp.zeros(z.shape[:-1], jnp.float32))
    acc = jnp.einsum('bcgts,bscd->btcgd', w, v_new.astype(jnp.float32))

    def step(carry, phys):
        acc, later = carry
        kp = cache_k[layer, phys]
        vp = cache_v[layer, phys]
        z = jnp.einsum('btcgd,bpcd->bcgtp', q, kp).astype(jnp.float32) * SB_SCALE
        w, later = sb_weights(z, True, later)
        acc = acc + jnp.einsum('bcgtp,bpcd->btcgd', w, vp.astype(jnp.float32))
        return (acc, later), None

    (acc, _), _ = lax.scan(step, (acc, later), page_table.T, reverse=True)
    return acc.astype(q.dtype).reshape(B, T, SB_HEADS * SB_DIM)


def mem_kv(mem, g, w_mk, w_mv):
    B, M, _ = mem.shape
    m = rmsnorm(mem, g)
    return ((m @ w_mk).reshape(B, M, MEM_HEADS, MEM_DIM),
            (m @ w_mv).reshape(B, M, MEM_HEADS, MEM_DIM))


def mem_attend(h, k, v, w_mq, w_mo):
    B, S, _ = h.shape
    q = (h @ w_mq).reshape(B, S, MEM_HEADS, MEM_DIM)
    s = jnp.einsum('bshd,bmhd->bhsm', q, k).astype(jnp.float32) * MEM_SCALE
    p = jax.nn.softmax(s, axis=-1).astype(v.dtype)
    o = jnp.einsum('bhsm,bmhd->bshd', p, v).reshape(B, S, MEM_HEADS * MEM_DIM)
    return o @ w_mo


def moe(h, w_router, b_router, w_gate, b_gate, w_up, b_up, w_down, b_down):
    B, S, D = h.shape
    x = h.reshape(-1, D)
    T = x.shape[0]
    logits = (x @ w_router).astype(jnp.float32) + b_router.astype(jnp.float32)
    top_val, top_idx = lax.top_k(logits, TOP_K)
    gates = jax.nn.softmax(top_val, axis=-1)
    M = T * TOP_K
    flat_e = top_idx.reshape(-1)
    order = jnp.argsort(flat_e)
    sorted_e = flat_e[order]
    sorted_tok = order // TOP_K
    counts = jnp.zeros((N_EXPERTS,), jnp.int32).at[flat_e].add(1)
    padded = (counts + MOE_BLOCK - 1) // MOE_BLOCK * MOE_BLOCK
    pad_end = jnp.cumsum(padded)
    pad_start = pad_end - padded
    grp_start = jnp.cumsum(counts) - counts
    dest = pad_start[sorted_e] + jnp.arange(M) - grp_start[sorted_e]
    n_blocks = -(-M // MOE_BLOCK) + N_EXPERTS
    slot_tok = jnp.full((n_blocks * MOE_BLOCK,), T, jnp.int32).at[dest].set(sorted_tok)
    block_e = jnp.minimum(jnp.searchsorted(pad_end, jnp.arange(n_blocks) * MOE_BLOCK, side='right'),
                          N_EXPERTS - 1)
    x_pad = jnp.concatenate([x, jnp.zeros((1, D), x.dtype)], axis=0)
    xb = x_pad[slot_tok].reshape(n_blocks, MOE_BLOCK, D)

    def expert_block(args):
        xe, e = args
        g = jnp.minimum(xe @ w_gate[e] + b_gate[e], SWIGLU_LIMIT)
        u = jnp.clip(xe @ w_up[e] + b_up[e], -SWIGLU_LIMIT, SWIGLU_LIMIT)
        a = g * jax.nn.sigmoid(SWIGLU_ALPHA * g) * (u + 1.0)
        return a @ w_down[e] + b_down[e]

    yb = lax.map(expert_block, (xb, block_e)).reshape(-1, D)
    y_sorted = yb[dest] * gates.reshape(-1)[order][:, None].astype(yb.dtype)
    return jnp.zeros((T, D), yb.dtype).at[sorted_tok].add(y_sorted).reshape(B, S, D).astype(h.dtype)


def setup_inputs(seed: int = 0) -> dict:
    key = jax.random.key(seed)
    ks = iter(jax.random.split(key, 48))
    f32 = jnp.float32
    L = DEPTH
    n_pages = PAST_LEN // PAGE_SIZE
    n_phys = (DEC_BATCH * n_pages * 5) // 4

    def nrm(shape, scale=1.0):
        r = jax.random.normal(next(ks), shape, f32)
        return r if scale == 1.0 else r * scale

    def gain(shape):
        return 1.0 + 0.05 * nrm(shape)

    x_prompt = nrm((BATCH, SEQ, D_MODEL))
    x_sample = nrm((DEC_BATCH, DEC_SEQ, D_MODEL))
    mem_prompt = nrm((BATCH, MEM_TOKENS, D_MODEL))
    cache_mla_latent = nrm((L, n_phys, PAGE_SIZE, KV_RANK))
    cache_mla_krope = nrm((L, n_phys, PAGE_SIZE, ROPE_DIM))
    cache_sb_k = nrm((L, n_phys, PAGE_SIZE, SB_KV_HEADS, SB_DIM))
    cache_sb_v = nrm((L, n_phys, PAGE_SIZE, SB_KV_HEADS, SB_DIM))
    cache_mem_k = nrm((L, DEC_BATCH, MEM_TOKENS, MEM_HEADS, MEM_DIM))
    cache_mem_v = nrm((L, DEC_BATCH, MEM_TOKENS, MEM_HEADS, MEM_DIM))
    perm = jax.random.permutation(next(ks), n_phys)
    page_table = perm[: DEC_BATCH * n_pages].reshape(DEC_BATCH, n_pages).astype(jnp.int32)
    return {
        'x_prompt': x_prompt,
        'x_sample': x_sample,
        'mem_prompt': mem_prompt,
        'cache_mla_latent': cache_mla_latent,
        'cache_mla_krope': cache_mla_krope,
        'cache_sb_k': cache_sb_k,
        'cache_sb_v': cache_sb_v,
        'cache_mem_k': cache_mem_k,
        'cache_mem_v': cache_mem_v,
        'page_table': page_table,
        'attn_norm': gain((L, D_MODEL)),
        'w_in': nrm((L, D_MODEL, IN_WIDTH), D_MODEL ** -0.5),
        'q_norm': gain((L, Q_RANK)),
        'w_uq': nrm((L, Q_RANK, MLA_HEADS * (NOPE_DIM + ROPE_DIM)), Q_RANK ** -0.5),
        'kv_norm': gain((L, KV_RANK)),
        'w_uk': nrm((L, KV_RANK, MLA_HEADS, NOPE_DIM), KV_RANK ** -0.5),
        'w_uv': nrm((L, KV_RANK, MLA_HEADS, V_DIM), KV_RANK ** -0.5),
        'w_o': nrm((L, MIX_WIDTH, D_MODEL), MIX_WIDTH ** -0.5),
        'xattn_norm': gain((L, D_MODEL)),
        'mem_norm': gain((L, D_MODEL)),
        'w_mq': nrm((L, D_MODEL, MEM_HEADS * MEM_DIM), D_MODEL ** -0.5),
        'w_mk': nrm((L, D_MODEL, MEM_HEADS * MEM_DIM), D_MODEL ** -0.5),
        'w_mv': nrm((L, D_MODEL, MEM_HEADS * MEM_DIM), D_MODEL ** -0.5),
        'w_mo': nrm((L, MEM_HEADS * MEM_DIM, D_MODEL), (MEM_HEADS * MEM_DIM) ** -0.5),
        'moe_norm': gain((L, D_MODEL)),
        'w_router': nrm((L, D_MODEL, N_EXPERTS), D_MODEL ** -0.5),
        'b_router': nrm((L, N_EXPERTS), 0.01),
        'w_gate': nrm((L, N_EXPERTS, D_MODEL, D_FF), D_MODEL ** -0.5),
        'b_gate': nrm((L, N_EXPERTS, D_FF), 0.01),
        'w_up': nrm((L, N_EXPERTS, D_MODEL, D_FF), D_MODEL ** -0.5),
        'b_up': nrm((L, N_EXPERTS, D_FF), 0.01),
        'w_down': nrm((L, N_EXPERTS, D_FF, D_MODEL), D_FF ** -0.5),
        'b_down': nrm((L, N_EXPERTS, D_MODEL), 0.01),
        'final_norm': gain((D_MODEL,)),
    }


def reference(x_prompt, x_sample, mem_prompt, cache_mla_latent, cache_mla_krope, cache_sb_k, cache_sb_v,
              cache_mem_k, cache_mem_v, page_table, attn_norm, w_in, q_norm, w_uq, kv_norm, w_uk, w_uv, w_o,
              xattn_norm, mem_norm, w_mq, w_mk, w_mv, w_mo, moe_norm, w_router, b_router, w_gate, b_gate,
              w_up, b_up, w_down, b_down, final_norm):
    S = x_prompt.shape[1]
    T = x_sample.shape[1]
    past_len = page_table.shape[1] * cache_mla_latent.shape[2]
    pos_p = jnp.arange(S)
    pos_s = past_len + jnp.arange(T)
    hp, hs = x_prompt, x_sample
    lat_p, kpe_p, sbk_p, sbv_p, memk_p, memv_p = [], [], [], [], [], []
    lat_s, kpe_s, sbk_s, sbv_s = [], [], [], []
    for l in range(DEPTH):
        moe_w = (w_router[l], b_router[l], w_gate[l], b_gate[l], w_up[l], b_up[l], w_down[l], b_down[l])
        qn, qp, lat, kpe, sq, sk, sv = project_mixers(rmsnorm(hp, attn_norm[l]), pos_p, w_in[l], q_norm[l], w_uq[l], kv_norm[l])
        mix = jnp.concatenate([mla_prompt(qn, qp, lat, kpe, w_uk[l], w_uv[l]), sb_prompt(sq, sk, sv)], axis=-1)
        hp = hp + mix @ w_o[l]
        mk, mv = mem_kv(mem_prompt, mem_norm[l], w_mk[l], w_mv[l])
        hp = hp + mem_attend(rmsnorm(hp, xattn_norm[l]), mk, mv, w_mq[l], w_mo[l])
        hp = hp + moe(rmsnorm(hp, moe_norm[l]), *moe_w)
        lat_p.append(lat); kpe_p.append(kpe); sbk_p.append(sk); sbv_p.append(sv)
        memk_p.append(mk); memv_p.append(mv)
        qn, qp, lat, kpe, sq, sk, sv = project_mixers(rmsnorm(hs, attn_norm[l]), pos_s, w_in[l], q_norm[l], w_uq[l], kv_norm[l])
        mix = jnp.concatenate([
            mla_decode(qn, qp, lat, kpe, cache_mla_latent, cache_mla_krope, l, page_table, w_uk[l], w_uv[l]),
            sb_decode(sq, sk, sv, cache_sb_k, cache_sb_v, l, page_table)], axis=-1)
        hs = hs + mix @ w_o[l]
        hs = hs + mem_attend(rmsnorm(hs, xattn_norm[l]), cache_mem_k[l], cache_mem_v[l], w_mq[l], w_mo[l])
        hs = hs + moe(rmsnorm(hs, moe_norm[l]), *moe_w)
        lat_s.append(lat); kpe_s.append(kpe); sbk_s.append(sk); sbv_s.append(sv)
    y_prompt = rmsnorm(hp, final_norm)
    y_sample = rmsnorm(hs, final_norm)
    return (y_prompt, y_sample,
            jnp.stack(lat_p), jnp.stack(kpe_p), jnp.stack(sbk_p), jnp.stack(sbv_p),
            jnp.stack(memk_p), jnp.stack(memv_p),
            jnp.stack(lat_s), jnp.stack(kpe_s), jnp.stack(sbk_s), jnp.stack(sbv_s))
```

```python
import functools

import jax
import jax.numpy as jnp
from jax import lax
from jax.experimental import pallas as pl
from jax.experimental.pallas import tpu as pltpu

F32 = jnp.float32
BF16 = jnp.bfloat16
I32 = jnp.int32

MLA_HEADS = 8
Q_RANK = 512
KV_RANK = 512
NOPE_DIM = 128
ROPE_DIM = 64
V_DIM = 128
ROPE_THETA = 10000.0
SB_HEADS = 8
SB_KV_HEADS = 2
SB_GROUP = SB_HEADS // SB_KV_HEADS
SB_DIM = 128
MEM_HEADS = 4
MEM_DIM = 128
N_EXPERTS = 32
TOP_K = 4
SWIGLU_LIMIT = 7.0
SWIGLU_ALPHA = 1.702
EPS = 1e-6
NEG_INF = -1e30
MLA_SCALE = (NOPE_DIM + ROPE_DIM) ** -0.5
SB_SCALE = SB_DIM ** -0.5
MEM_SCALE = MEM_DIM ** -0.5

LANES = 128
QK_PAD = 2 * LANES
VMEM_LIMIT = 56 * 1024 * 1024

SB_SKIP_LOG = -120.0

ROW_TILE = 320
ATT_TILE = 256
MOE_TOK_TILE = 128
MOE_CHUNK = 256
MOE_SUPER = 2048
MOE_FF_TILE = 256
DEC_PAGES = 8

_NT = (((1,), (1,)), ((), ()))


def _rms(x, g):
    return x * lax.rsqrt(jnp.mean(x * x, axis=-1, keepdims=True) + EPS) * g


def _dot(a, b):
    return jnp.dot(a, b, preferred_element_type=F32)


def _dot_nt(a, b):
    return lax.dot_general(a, b, _NT, preferred_element_type=F32)


def _const_spec(shape):
    nd = len(shape)
    return pl.BlockSpec(shape, lambda *_: (0,) * nd)


def _params(sem, vmem=VMEM_LIMIT):
    return pltpu.CompilerParams(dimension_semantics=sem, vmem_limit_bytes=vmem)


def _proj_body(x_ref, ga_ref, wcat_ref, qn_ref, wq_ref, kvn_ref, wuk_ref, wuv_ref, cos_ref, sin_ref,
               qf_ref, kf_ref, v_ref, lat_ref, kpe_ref, sbq_ref, sbk_ref, sbv_ref, sbkb_ref, sbvb_ref):
    xn = _rms(x_ref[...], ga_ref[...]).astype(BF16)
    p = _dot(xn, wcat_ref[...])
    cos = cos_ref[...]
    sin = sin_ref[...]
    o_cq, o_ckv, o_sq, o_sk, o_sv, o_kp, o_kr = 0, 512, 1024, 2048, 2304, 2560, 2688

    cqn = _rms(p[:, o_cq:o_cq + Q_RANK], qn_ref[...]).astype(BF16)
    q = _dot(cqn, wq_ref[...])
    hw = MLA_HEADS * LANES
    for h in range(MLA_HEADS):
        sl = slice(h * LANES, (h + 1) * LANES)
        q_pe = q[:, hw + h * LANES: hw + (h + 1) * LANES] * cos + q[:, 2 * hw + h * LANES: 2 * hw + (h + 1) * LANES] * sin
        qf_ref[:, h * QK_PAD: h * QK_PAD + LANES] = (q[:, sl] * MLA_SCALE).astype(BF16)
        qf_ref[:, h * QK_PAD + LANES: (h + 1) * QK_PAD] = (q_pe * MLA_SCALE).astype(BF16)

    lat = _rms(p[:, o_ckv:o_ckv + KV_RANK], kvn_ref[...])
    lat_ref[...] = lat
    latb = lat.astype(BF16)
    kn = _dot(latb, wuk_ref[...])
    v_ref[...] = _dot(latb, wuv_ref[...]).astype(BF16)
    kpe = p[:, o_kp:o_kp + LANES] * cos + p[:, o_kr:o_kr + LANES] * sin
    kpe_ref[...] = kpe
    kpeb = kpe.astype(BF16)
    for h in range(MLA_HEADS):
        kf_ref[:, h * QK_PAD: h * QK_PAD + LANES] = kn[:, h * LANES:(h + 1) * LANES].astype(BF16)
        kf_ref[:, h * QK_PAD + LANES: (h + 1) * QK_PAD] = kpeb

    sbq_ref[...] = (p[:, o_sq:o_sq + SB_HEADS * SB_DIM] * SB_SCALE).astype(BF16)
    sbk = p[:, o_sk:o_sk + SB_KV_HEADS * SB_DIM]
    sbv = p[:, o_sv:o_sv + SB_KV_HEADS * SB_DIM]
    sbk_ref[...] = sbk
    sbv_ref[...] = sbv
    sbkb_ref[...] = sbk.astype(BF16)
    sbvb_ref[...] = sbv.astype(BF16)


def _project(x, ga, wcat, qn, wq, kvn, wuk, wuv, cos, sin):
    m, d = x.shape
    tm = ROW_TILE
    assert m % tm == 0
    row = lambda w: pl.BlockSpec((tm, w), lambda i: (i, 0))
    kvw = SB_KV_HEADS * SB_DIM
    outs = [((m, MLA_HEADS * QK_PAD), BF16), ((m, MLA_HEADS * QK_PAD), BF16), ((m, MLA_HEADS * V_DIM), BF16),
            ((m, KV_RANK), F32), ((m, LANES), F32), ((m, SB_HEADS * SB_DIM), BF16),
            ((m, kvw), F32), ((m, kvw), F32), ((m, kvw), BF16), ((m, kvw), BF16)]
    consts = [ga, wcat, qn, wq, kvn, wuk, wuv]
    return pl.pallas_call(
        _proj_body,
        grid=(m // tm,),
        in_specs=[row(d)] + [_const_spec(c.shape) for c in consts] + [row(LANES), row(LANES)],
        out_specs=[row(s[1]) for s, _ in outs],
        out_shape=[jax.ShapeDtypeStruct(s, dt) for s, dt in outs],
        name="proj_in",
        compiler_params=_params(("arbitrary",)),
    )(x, *consts, cos, sin)


def _mla_prompt_body(q_ref, k_ref, v_ref, o_ref, *, t):
    qi = pl.program_id(1)
    q = q_ref[...]

    def block(j, carry, masked):
        m, l, acc = carry
        start = pl.multiple_of(j * t, t)
        s = _dot_nt(q, k_ref[pl.ds(start, t), :])
        if masked:
            row = lax.broadcasted_iota(I32, (t, t), 0)
            col = lax.broadcasted_iota(I32, (t, t), 1)
            s = jnp.where(col <= row, s, NEG_INF)
        m_new = jnp.maximum(m, jnp.max(s, axis=-1, keepdims=True))
        alpha = jnp.exp(m - m_new)
        p = jnp.exp(s - m_new)
        l = alpha * l + jnp.sum(p, axis=-1, keepdims=True)
        acc = alpha * acc + _dot(p.astype(BF16), v_ref[pl.ds(start, t), :])
        return m_new, l, acc

    init = (jnp.full((t, 1), NEG_INF, F32), jnp.zeros((t, 1), F32), jnp.zeros((t, V_DIM), F32))
    carry = lax.fori_loop(0, qi, lambda j, c: block(j, c, False), init)
    _, l, acc = block(qi, carry, True)
    o_ref[...] = (acc / l).astype(BF16)


def _mla_prompt(qf, kf, vv, s_len):
    t = ATT_TILE
    assert s_len % t == 0
    return pl.pallas_call(
        functools.partial(_mla_prompt_body, t=t),
        grid=(MLA_HEADS, s_len // t),
        in_specs=[pl.BlockSpec((t, QK_PAD), lambda h, i: (i, h)),
                  pl.BlockSpec((s_len, QK_PAD), lambda h, i: (0, h)),
                  pl.BlockSpec((s_len, V_DIM), lambda h, i: (0, h))],
        out_specs=pl.BlockSpec((t, V_DIM), lambda h, i: (i, h)),
        out_shape=jax.ShapeDtypeStruct((s_len, MLA_HEADS * V_DIM), BF16),
        name="mla_prompt",
        compiler_params=_params(("arbitrary", "arbitrary")),
    )(qf, kf, vv)


def _sb_logs(z):
    t = jnp.log1p(jnp.exp(-jnp.abs(z)))
    return jnp.minimum(z, 0.0) - t, -jnp.maximum(z, 0.0) - t


def _later_mask(n):
    return (lax.broadcasted_iota(I32, (n, n), 0) > lax.broadcasted_iota(I32, (n, n), 1)).astype(BF16)


def _suffix_sums(x, u):
    hi = x.astype(BF16)
    lo = (x - hi.astype(F32)).astype(BF16)
    return _dot(hi, u) + _dot(lo, u)


def _sb_prompt_body(q_ref, k_ref, v_ref, o_ref, *, t):
    qi = pl.program_id(1)
    rows = SB_GROUP * t
    q = jnp.concatenate([q_ref[:, g * SB_DIM:(g + 1) * SB_DIM] for g in range(SB_GROUP)], axis=0)
    u = _later_mask(t)

    def block(j, later, acc, masked):
        start = pl.multiple_of(j * t, t)
        z = _dot_nt(q, k_ref[pl.ds(start, t), :])
        log_b, log_1m = _sb_logs(z)
        if masked:
            row = lax.broadcasted_iota(I32, (rows, t), 0) & (t - 1)
            col = lax.broadcasted_iota(I32, (rows, t), 1)
            valid = col < row
            log_1m = jnp.where(valid, log_1m, 0.0)
        w = jnp.exp(log_b + later + _suffix_sums(log_1m, u))
        if masked:
            w = jnp.where(valid, w, 0.0)
        acc = acc + _dot(w.astype(BF16), v_ref[pl.ds(start, t), :])
        return later + jnp.sum(log_1m, axis=-1, keepdims=True), acc

    later, acc = block(qi, jnp.zeros((rows, 1), F32), jnp.zeros((rows, SB_DIM), F32), True)

    def cond(c):
        j, later, _ = c
        return jnp.logical_and(j >= 0, jnp.max(later) > SB_SKIP_LOG)

    def body(c):
        j, later, acc = c
        later, acc = block(j, later, acc, False)
        return j - 1, later, acc

    _, _, acc = lax.while_loop(cond, body, (qi - 1, later, acc))
    for g in range(SB_GROUP):
        o_ref[:, g * SB_DIM:(g + 1) * SB_DIM] = acc[g * t:(g + 1) * t].astype(BF16)


def _sb_prompt(sbq, sbkb, sbvb, s_len):
    t = ATT_TILE
    gw = SB_GROUP * SB_DIM
    return pl.pallas_call(
        functools.partial(_sb_prompt_body, t=t),
        grid=(SB_KV_HEADS, s_len // t),
        in_specs=[pl.BlockSpec((t, gw), lambda c, i: (i, c)),
                  pl.BlockSpec((s_len, SB_DIM), lambda c, i: (0, c)),
                  pl.BlockSpec((s_len, SB_DIM), lambda c, i: (0, c))],
        out_specs=pl.BlockSpec((t, gw), lambda c, i: (i, c)),
        out_shape=jax.ShapeDtypeStruct((s_len, SB_HEADS * SB_DIM), BF16),
        name="sb_prompt",
        compiler_params=_params(("arbitrary", "arbitrary")),
    )(sbq, sbkb, sbvb)


def _head_matmul_body(a_ref, w_ref, o_ref, *, nt):
    w = w_ref[...]
    o_ref[...] = (_dot_nt(a_ref[...], w) if nt else _dot(a_ref[...], w)).astype(o_ref.dtype)


def _per_head_matmul(a, w, row_block, col_stride, nt):
    nh, wk, wn = w.shape
    a_width, out_w = (wn, wk) if nt else (wk, wn)
    bd = MOE_TOK_TILE
    return pl.pallas_call(
        functools.partial(_head_matmul_body, nt=nt),
        grid=(nh,),
        in_specs=[pl.BlockSpec((bd, a_width), lambda h: (row_block, h * col_stride)),
                  pl.BlockSpec((None, wk, wn), lambda h: (h, 0, 0))],
        out_specs=pl.BlockSpec((bd, out_w), lambda h: (0, h)),
        out_shape=jax.ShapeDtypeStruct((bd, nh * out_w), BF16),
        name="per_head_matmul",
        compiler_params=_params(("arbitrary",)),
    )(a, w)


def _mla_decode_body(pt_ref, ql_ref, qp_ref, ln_ref, kn_ref, *refs, pp):
    del pt_ref
    lat_refs, kr_refs = refs[:pp], refs[pp:2 * pp]
    o_ref, m_sc, l_sc, acc_sc = refs[2 * pp:]
    p = pl.program_id(1)
    ql = ql_ref[0]
    qp = qp_ref[0]

    @pl.when(p == 0)
    def _():
        ln = ln_ref[0]
        s = (jnp.sum(ql.astype(F32) * ln, axis=-1, keepdims=True)
             + jnp.sum(qp.astype(F32) * kn_ref[0], axis=-1, keepdims=True))
        m_sc[...] = s
        l_sc[...] = jnp.ones_like(s)
        acc_sc[...] = jnp.broadcast_to(ln, acc_sc.shape)

    lat = jnp.concatenate([r[...].astype(BF16) for r in lat_refs], axis=0)
    kr = jnp.concatenate([r[...].astype(BF16) for r in kr_refs], axis=0)
    s = _dot_nt(ql, lat) + _dot_nt(qp, kr)
    m_prev = m_sc[...]
    m_new = jnp.maximum(m_prev, jnp.max(s, axis=-1, keepdims=True))
    alpha = jnp.exp(m_prev - m_new)
    pr = jnp.exp(s - m_new)
    l_new = alpha * l_sc[...] + jnp.sum(pr, axis=-1, keepdims=True)
    acc = alpha * acc_sc[...] + _dot(pr.astype(BF16), lat)
    m_sc[...] = m_new
    l_sc[...] = l_new
    acc_sc[...] = acc

    @pl.when(p == pl.num_programs(1) - 1)
    def _():
        o_ref[0] = (acc / l_new).astype(BF16)


def _mla_decode(page_table, qlat, qpe, lat_new, kpe_new, cache_lat, cache_kr):
    bd, n_pages = page_table.shape
    pp = DEC_PAGES
    assert n_pages % pp == 0
    page = cache_lat.shape[2]
    per_seq = lambda w: pl.BlockSpec((1, MLA_HEADS, w), lambda b, p, pt: (b, 0, 0))
    new_tok = lambda w: pl.BlockSpec((1, 1, w), lambda b, p, pt: (b, 0, 0))
    paged = lambda w, i: pl.BlockSpec((None, None, page, w), lambda b, p, pt: (0, pt[b, p * pp + i], 0, 0))
    return pl.pallas_call(
        functools.partial(_mla_decode_body, pp=pp),
        grid_spec=pltpu.PrefetchScalarGridSpec(
            num_scalar_prefetch=1,
            grid=(bd, n_pages // pp),
            in_specs=[per_seq(KV_RANK), per_seq(ROPE_DIM), new_tok(KV_RANK), new_tok(ROPE_DIM)]
            + [paged(KV_RANK, i) for i in range(pp)] + [paged(ROPE_DIM, i) for i in range(pp)],
            out_specs=per_seq(KV_RANK),
            scratch_shapes=[pltpu.VMEM((MLA_HEADS, 1), F32), pltpu.VMEM((MLA_HEADS, 1), F32),
                            pltpu.VMEM((MLA_HEADS, KV_RANK), F32)]),
        out_shape=jax.ShapeDtypeStruct((bd, MLA_HEADS, KV_RANK), BF16),
        name="mla_decode",
        compiler_params=_params(("arbitrary", "arbitrary")),
    )(page_table, qlat, qpe, lat_new, kpe_new, *([cache_lat] * pp), *([cache_kr] * pp))


def _sb_decode_body(pt_ref, q_ref, *refs, pp, page):
    del pt_ref
    k_refs, v_refs = refs[:pp], refs[pp:2 * pp]
    o_ref, later_sc, acc_sc = refs[2 * pp:]
    p = pl.program_id(1)

    @pl.when(p == 0)
    def _():
        later_sc[...] = jnp.zeros_like(later_sc)
        acc_sc[...] = jnp.zeros_like(acc_sc)

    q = q_ref[0]
    n = pp * page
    head_of_kv0 = lax.broadcasted_iota(I32, (SB_HEADS, 1), 0) < SB_GROUP

    def per_kv_head(refs_, c):
        return jnp.concatenate([r[:, c * SB_DIM:(c + 1) * SB_DIM].astype(BF16) for r in refs_], axis=0)

    z = jnp.where(head_of_kv0, _dot_nt(q, per_kv_head(k_refs, 0)), _dot_nt(q, per_kv_head(k_refs, 1)))
    log_b, log_1m = _sb_logs(z)
    stacked = jnp.concatenate([log_1m[:, i * page:(i + 1) * page] for i in range(pp)], axis=0)
    inside = _suffix_sums(stacked, _later_mask(page))
    total = jnp.sum(stacked, axis=-1, keepdims=True)
    run = later_sc[...]
    ws = [None] * pp
    for i in reversed(range(pp)):
        rs = slice(i * SB_HEADS, (i + 1) * SB_HEADS)
        ws[i] = jnp.exp(log_b[:, i * page:(i + 1) * page] + run + inside[rs])
        run = run + total[rs]
    w = jnp.concatenate(ws, axis=1).astype(BF16)
    del n
    acc = acc_sc[...] + jnp.where(head_of_kv0, _dot(w, per_kv_head(v_refs, 0)), _dot(w, per_kv_head(v_refs, 1)))
    later_sc[...] = run
    acc_sc[...] = acc

    @pl.when(p == pl.num_programs(1) - 1)
    def _():
        o_ref[0] = acc.astype(BF16)


def _sb_decode(page_table, q, cache_k, cache_v):
    bd, n_pages = page_table.shape
    pp = DEC_PAGES
    page = cache_k.shape[2]
    kvw = SB_KV_HEADS * SB_DIM
    per_seq = pl.BlockSpec((1, SB_HEADS, SB_DIM), lambda b, p, pt: (b, 0, 0))
    paged = lambda i: pl.BlockSpec((None, None, page, kvw),
                                   lambda b, p, pt: (0, pt[b, n_pages - (p + 1) * pp + i], 0, 0))
    return pl.pallas_call(
        functools.partial(_sb_decode_body, pp=pp, page=page),
        grid_spec=pltpu.PrefetchScalarGridSpec(
            num_scalar_prefetch=1,
            grid=(bd, n_pages // pp),
            in_specs=[per_seq] + [paged(i) for i in range(pp)] * 2,
            out_specs=per_seq,
            scratch_shapes=[pltpu.VMEM((SB_HEADS, 1), F32), pltpu.VMEM((SB_HEADS, SB_DIM), F32)]),
        out_shape=jax.ShapeDtypeStruct((bd, SB_HEADS, SB_DIM), BF16),
        name="sb_decode",
        compiler_params=_params(("arbitrary", "arbitrary")),
    )(page_table, q, *([cache_k] * pp), *([cache_v] * pp))


def _wo_body(x_ref, mla_ref, sb_ref, wo_a_ref, wo_b_ref, gx_ref, wmq_ref, h1_ref, qm_ref):
    h1 = x_ref[...] + _dot(mla_ref[...], wo_a_ref[...]) + _dot(sb_ref[...], wo_b_ref[...])
    h1_ref[...] = h1
    hn = _rms(h1, gx_ref[...]).astype(BF16)
    qm_ref[...] = (_dot(hn, wmq_ref[...]) * MEM_SCALE).astype(BF16)


def _out_proj(x, mla, sb, wo_a, wo_b, gx, wmq):
    m, d = x.shape
    tm = ROW_TILE
    row = lambda w: pl.BlockSpec((tm, w), lambda i: (i, 0))
    mw = MEM_HEADS * MEM_DIM
    return pl.pallas_call(
        _wo_body,
        grid=(m // tm,),
        in_specs=[row(d), row(mla.shape[1]), row(sb.shape[1])] + [_const_spec(c.shape) for c in (wo_a, wo_b, gx, wmq)],
        out_specs=[row(d), row(mw)],
        out_shape=[jax.ShapeDtypeStruct((m, d), F32), jax.ShapeDtypeStruct((m, mw), BF16)],
        name="out_proj",
        compiler_params=_params(("arbitrary",)),
    )(x, mla, sb, wo_a, wo_b, gx, wmq)


def _mem_kv_body(mem_ref, g_ref, wk_ref, wv_ref, k_ref, v_ref):
    m = _rms(mem_ref[...], g_ref[...]).astype(BF16)
    k_ref[...] = _dot(m, wk_ref[...])
    v_ref[...] = _dot(m, wv_ref[...])


def _mem_kv(mem, g, wk, wv):
    n = mem.shape[0]
    mw = MEM_HEADS * MEM_DIM
    args = (mem, g, wk, wv)
    return pl.pallas_call(
        _mem_kv_body,
        grid=(1,),
        in_specs=[_const_spec(a.shape) for a in args],
        out_specs=[_const_spec((n, mw))] * 2,
        out_shape=[jax.ShapeDtypeStruct((n, mw), F32)] * 2,
        name="mem_kv",
        compiler_params=_params(("arbitrary",)),
    )(*args)


def _mem_prompt_body(q_ref, k_ref, v_ref, o_ref):
    k = k_ref[...].astype(BF16)
    v = v_ref[...].astype(BF16)
    for h in range(MEM_HEADS):
        sl = slice(h * MEM_DIM, (h + 1) * MEM_DIM)
        s = _dot_nt(q_ref[:, sl], k[:, sl])
        e = jnp.exp(s - jnp.max(s, axis=-1, keepdims=True))
        p = e / jnp.sum(e, axis=-1, keepdims=True)
        o_ref[:, sl] = _dot(p.astype(BF16), v[:, sl]).astype(BF16)


def _mem_prompt(qm, mk, mv, s_len):
    tm = ATT_TILE
    mw = MEM_HEADS * MEM_DIM
    row = pl.BlockSpec((tm, mw), lambda i: (i, 0))
    return pl.pallas_call(
        _mem_prompt_body,
        grid=(s_len // tm,),
        in_specs=[row, _const_spec(mk.shape), _const_spec(mv.shape)],
        out_specs=row,
        out_shape=jax.ShapeDtypeStruct((s_len, mw), BF16),
        name="mem_prompt",
        compiler_params=_params(("arbitrary",)),
    )(qm, mk, mv)


def _mem_decode_body(q_ref, k_ref, v_ref, o_ref):
    prod = k_ref[0] * q_ref[0].astype(F32)
    v = v_ref[0]
    for h in range(MEM_HEADS):
        sl = slice(h * MEM_DIM, (h + 1) * MEM_DIM)
        s = jnp.sum(prod[:, sl], axis=-1, keepdims=True)
        e = jnp.exp(s - jnp.max(s, axis=0, keepdims=True))
        p = e / jnp.sum(e, axis=0, keepdims=True)
        o_ref[0, :, sl] = jnp.sum(p * v[:, sl], axis=0, keepdims=True).astype(BF16)


def _mem_decode(qm, cache_k, cache_v):
    bd = qm.shape[0]
    n, mw = cache_k.shape[1:]
    one = pl.BlockSpec((1, 1, mw), lambda b: (b, 0, 0))
    kv = pl.BlockSpec((1, n, mw), lambda b: (b, 0, 0))
    return pl.pallas_call(
        _mem_decode_body,
        grid=(bd,),
        in_specs=[one, kv, kv],
        out_specs=one,
        out_shape=jax.ShapeDtypeStruct((bd, 1, mw), BF16),
        name="mem_decode",
        compiler_params=_params(("arbitrary",)),
    )(qm, cache_k, cache_v)


def _route_body(h1_ref, o_ref, wmo_ref, gm_ref, wr_ref, br_ref,
                h2_ref, hn_ref, idx_ref, gate_ref, rank_ref, cnt_ref, run_sc):
    @pl.when(pl.program_id(0) == 0)
    def _():
        run_sc[...] = jnp.zeros_like(run_sc)

    h2 = h1_ref[...] + _dot(o_ref[...], wmo_ref[...])
    h2_ref[...] = h2
    hn = _rms(h2, gm_ref[...])
    hn_ref[...] = hn
    logits = jnp.dot(hn, wr_ref[...], precision=lax.Precision.HIGHEST, preferred_element_type=F32) + br_ref[...]

    tm = logits.shape[0]
    lane = lax.broadcasted_iota(I32, (tm, N_EXPERTS), 1)
    work = logits
    vals, idxs = [], []
    for _ in range(TOP_K):
        mx = jnp.max(work, axis=-1, keepdims=True)
        ix = jnp.min(jnp.where(work == mx, lane, N_EXPERTS), axis=-1, keepdims=True)
        vals.append(mx)
        idxs.append(ix)
        work = jnp.where(lane == ix, -jnp.inf, work)
    es = [jnp.exp(v - vals[0]) for v in vals]
    den = es[0] + es[1] + es[2] + es[3]

    chosen = (lane == idxs[0]) | (lane == idxs[1]) | (lane == idxs[2]) | (lane == idxs[3])
    onehot = jnp.where(chosen, 1.0, 0.0)
    earlier = (lax.broadcasted_iota(I32, (tm, tm), 0) > lax.broadcasted_iota(I32, (tm, tm), 1)).astype(BF16)
    rank = run_sc[...] + _dot(earlier, onehot.astype(BF16))
    for k in range(TOP_K):
        idx_ref[:, k:k + 1] = idxs[k]
        gate_ref[:, k:k + 1] = es[k] / den
        rank_ref[:, k:k + 1] = jnp.sum(jnp.where(lane == idxs[k], rank, 0.0), axis=-1, keepdims=True).astype(I32)
    run_sc[...] = run_sc[...] + jnp.sum(onehot, axis=0, keepdims=True)
    cnt_ref[...] = run_sc[...]


def _route(h1, o, wmo, gm, wr, br):
    m, d = h1.shape
    tm = MOE_TOK_TILE
    row = lambda w: pl.BlockSpec((tm, w), lambda i: (i, 0))
    outs = [((m, d), F32), ((m, d), F32), ((m, TOP_K), I32), ((m, TOP_K), F32), ((m, TOP_K), I32)]
    return pl.pallas_call(
        _route_body,
        grid=(m // tm,),
        in_specs=[row(d), row(o.shape[1])] + [_const_spec(c.shape) for c in (wmo, gm, wr, br)],
        out_specs=[row(s[1]) for s, _ in outs] + [_const_spec((1, N_EXPERTS))],
        out_shape=[jax.ShapeDtypeStruct(s, dt) for s, dt in outs] + [jax.ShapeDtypeStruct((1, N_EXPERTS), F32)],
        scratch_shapes=[pltpu.VMEM((1, N_EXPERTS), F32)],
        name="moe_route",
        compiler_params=_params(("arbitrary",)),
    )(h1, o, wmo, gm, wr, br)


def _dispatch_body(dest_ref, hn_ref, zero_ref, xs_ref, sem):
    del zero_ref
    tm = hn_ref.shape[0]

    def row_copy(t, d):
        return pltpu.make_async_copy(hn_ref.at[pl.ds(t, 1)], xs_ref.at[pl.ds(d, 1)], sem)

    def issue(t, c):
        for k in range(TOP_K):
            row_copy(t, dest_ref[t * TOP_K + k]).start()
        return c

    def drain(t, c):
        for k in range(TOP_K):
            row_copy(t, dest_ref[t * TOP_K + k]).wait()
        return c

    lax.fori_loop(0, tm, issue, 0)
    lax.fori_loop(0, tm, drain, 0)


def _dispatch(dest_flat, hn, n_slots):
    m, d = hn.shape
    tm = MOE_TOK_TILE
    zeros = jnp.zeros((n_slots, d), hn.dtype)
    return pl.pallas_call(
        _dispatch_body,
        grid=(m // tm,),
        in_specs=[pl.BlockSpec((tm * TOP_K,), lambda i: (i,), memory_space=pltpu.SMEM),
                  pl.BlockSpec((tm, d), lambda i: (i, 0)),
                  pl.BlockSpec(memory_space=pl.ANY)],
        out_specs=pl.BlockSpec(memory_space=pl.ANY),
        out_shape=jax.ShapeDtypeStruct((n_slots, d), hn.dtype),
        scratch_shapes=[pltpu.SemaphoreType.DMA(())],
        input_output_aliases={2: 0},
        name="moe_dispatch",
        compiler_params=_params(("arbitrary",)),
    )(dest_flat, hn, zeros)


def _expert_body(sb_e, sb_start, sb_nch, xs_ref, wg_ref, wu_ref, wd_ref, bg_ref, bu_ref, bd_ref, yb_ref,
                 x_sc, y_sc, stage_sc, wg_sc, wu_sc, wd_sc, sem):
    del sb_e
    i = pl.program_id(0)
    j = pl.program_id(1)
    nch = sb_nch[i]
    start = sb_start[i]
    ch = stage_sc.shape[0]

    def rows(c):
        return pl.ds(pl.multiple_of(c * ch, ch), ch)

    @pl.when(nch > 0)
    def _():
        @pl.when(j == 0)
        def _():
            def load(c, carry):
                cp = pltpu.make_async_copy(xs_ref.at[pl.ds(pl.multiple_of(start + c * ch, ch), ch)], stage_sc, sem)
                cp.start()
                cp.wait()
                x_sc[rows(c), :] = stage_sc[...].astype(BF16)
                y_sc[rows(c), :] = jnp.broadcast_to(bd_ref[...], stage_sc.shape)
                return carry
            lax.fori_loop(0, nch, load, 0)

        wg_sc[...] = wg_ref[...].astype(BF16)
        wu_sc[...] = wu_ref[...].astype(BF16)
        wd_sc[...] = wd_ref[...].astype(BF16)
        bg = bg_ref[...]
        bu = bu_ref[...]

        def chunk(c, carry):
            x = x_sc[rows(c), :]
            g = jnp.minimum(_dot(x, wg_sc[...]) + bg, SWIGLU_LIMIT)
            u = jnp.clip(_dot(x, wu_sc[...]) + bu, -SWIGLU_LIMIT, SWIGLU_LIMIT)
            a = g * jax.nn.sigmoid(SWIGLU_ALPHA * g) * (u + 1.0)
            y_sc[rows(c), :] += _dot(a.astype(BF16), wd_sc[...])
            return carry
        lax.fori_loop(0, nch, chunk, 0)

        @pl.when(j == pl.num_programs(1) - 1)
        def _():
            def store(c, carry):
                stage_sc[...] = y_sc[rows(c), :]
                cp = pltpu.make_async_copy(stage_sc, yb_ref.at[pl.ds(pl.multiple_of(start + c * ch, ch), ch)], sem)
                cp.start()
                cp.wait()
                return carry
            lax.fori_loop(0, nch, store, 0)


def _experts(sb_e, sb_start, sb_nch, xs, wg, wu, wd, bg, bu, bd):
    n_slots, d = xs.shape
    d_ff = wg.shape[2]
    tf = MOE_FF_TILE
    nj = d_ff // tf
    n_sb = sb_e.shape[0]

    def jj(i, j, n):
        return jnp.where(n[i] > 0, j, nj - 1)

    return pl.pallas_call(
        _expert_body,
        grid_spec=pltpu.PrefetchScalarGridSpec(
            num_scalar_prefetch=3,
            grid=(n_sb, nj),
            in_specs=[pl.BlockSpec(memory_space=pl.ANY),
                      pl.BlockSpec((None, d, tf), lambda i, j, e, s, n: (e[i], 0, jj(i, j, n))),
                      pl.BlockSpec((None, d, tf), lambda i, j, e, s, n: (e[i], 0, jj(i, j, n))),
                      pl.BlockSpec((None, tf, d), lambda i, j, e, s, n: (e[i], jj(i, j, n), 0)),
                      pl.BlockSpec((None, 1, tf), lambda i, j, e, s, n: (e[i], 0, jj(i, j, n))),
                      pl.BlockSpec((None, 1, tf), lambda i, j, e, s, n: (e[i], 0, jj(i, j, n))),
                      pl.BlockSpec((None, 1, d), lambda i, j, e, s, n: (e[i], 0, 0))],
            out_specs=pl.BlockSpec(memory_space=pl.ANY),
            scratch_shapes=[pltpu.VMEM((MOE_SUPER, d), BF16), pltpu.VMEM((MOE_SUPER, d), F32),
                            pltpu.VMEM((MOE_CHUNK, d), F32),
                            pltpu.VMEM((d, tf), BF16), pltpu.VMEM((d, tf), BF16), pltpu.VMEM((tf, d), BF16),
                            pltpu.SemaphoreType.DMA(())]),
        out_shape=jax.ShapeDtypeStruct((n_slots, d), F32),
        input_output_aliases={3: 0},
        name="moe_experts",
        compiler_params=_params(("arbitrary", "arbitrary")),
    )(sb_e, sb_start, sb_nch, xs, wg, wu, wd, bg, bu, bd)


def _combine_body(dest_ref, gate_ref, h2_ref, gf_ref, yb_ref, y_ref, buf, sem):
    tm = h2_ref.shape[0]

    def row_copy(t, k):
        return pltpu.make_async_copy(yb_ref.at[pl.ds(dest_ref[t * TOP_K + k], 1)], buf.at[k, pl.ds(t, 1)], sem)

    def issue(t, c):
        for k in range(TOP_K):
            row_copy(t, k).start()
        return c

    def drain(t, c):
        for k in range(TOP_K):
            row_copy(t, k).wait()
        return c

    lax.fori_loop(0, tm, issue, 0)
    lax.fori_loop(0, tm, drain, 0)
    gate = gate_ref[...]
    moe = gate[:, 0:1] * buf[0]
    for k in range(1, TOP_K):
        moe = moe + gate[:, k:k + 1] * buf[k]
    y_ref[...] = _rms(h2_ref[...] + moe, gf_ref[...])


def _combine(dest_flat, gates, h2, gf, yb):
    m, d = h2.shape
    tm = MOE_TOK_TILE
    return pl.pallas_call(
        _combine_body,
        grid=(m // tm,),
        in_specs=[pl.BlockSpec((tm * TOP_K,), lambda i: (i,), memory_space=pltpu.SMEM),
                  pl.BlockSpec((tm, TOP_K), lambda i: (i, 0)),
                  pl.BlockSpec((tm, d), lambda i: (i, 0)),
                  _const_spec(gf.shape),
                  pl.BlockSpec(memory_space=pl.ANY)],
        out_specs=pl.BlockSpec((tm, d), lambda i: (i, 0)),
        out_shape=jax.ShapeDtypeStruct((m, d), F32),
        scratch_shapes=[pltpu.VMEM((TOP_K, tm, d), F32), pltpu.SemaphoreType.DMA(())],
        name="moe_combine",
        compiler_params=_params(("arbitrary",)),
    )(dest_flat, gates, h2, gf, yb)


def _rot_cols(w):
    half = w.shape[-1] // 2
    return jnp.concatenate([-w[..., half:], w[..., :half]], axis=-1)


def _pad_lanes(w):
    return jnp.pad(w, [(0, 0)] * (w.ndim - 1) + [(0, LANES - w.shape[-1])])


def _super_blocks(counts, n_sb):
    padded = (counts + MOE_CHUNK - 1) // MOE_CHUNK * MOE_CHUNK
    pad_end = jnp.cumsum(padded)
    pad_start = pad_end - padded
    per_e = (padded + MOE_SUPER - 1) // MOE_SUPER
    sb_end = jnp.cumsum(per_e)
    i = jnp.arange(n_sb, dtype=I32)
    live = i < sb_end[-1]
    e = jnp.minimum(jnp.searchsorted(sb_end, i, side="right"), N_EXPERTS - 1).astype(I32)
    last_e = jnp.minimum(jnp.searchsorted(sb_end, sb_end[-1] - 1, side="right"), N_EXPERTS - 1).astype(I32)
    e = jnp.where(live, e, last_e)
    local = i - (sb_end[e] - per_e[e])
    rows = jnp.clip(padded[e] - local * MOE_SUPER, 0, MOE_SUPER)
    nch = jnp.where(live, rows // MOE_CHUNK, 0).astype(I32)
    start = jnp.where(live, pad_start[e] + local * MOE_SUPER, 0).astype(I32)
    return pad_start.astype(I32), e, start, nch


def kernel(x_prompt, x_sample, mem_prompt, cache_mla_latent, cache_mla_krope, cache_sb_k, cache_sb_v, cache_mem_k, cache_mem_v, page_table, attn_norm, w_in, q_norm, w_uq, kv_norm, w_uk, w_uv, w_o, xattn_norm, mem_norm, w_mq, w_mk, w_mv, w_mo, moe_norm, w_router, b_router, w_gate, b_gate, w_up, b_up, w_down, b_down, final_norm):
    depth = w_in.shape[0]
    assert depth == 1, "single-layer step"
    batch, s_len, d = x_prompt.shape
    bd, t_dec, _ = x_sample.shape
    assert batch == 1 and t_dec == 1 and bd == MOE_TOK_TILE
    n_pages = page_table.shape[1]
    page = cache_mla_latent.shape[2]
    past_len = n_pages * page
    m_tot = s_len + bd
    half = ROPE_DIM // 2
    row2 = lambda a: a.reshape(1, -1)

    wi = w_in[0]
    o1, o2, o3 = Q_RANK, Q_RANK + KV_RANK, Q_RANK + KV_RANK + ROPE_DIM
    o4 = o3 + SB_HEADS * SB_DIM
    o5 = o4 + SB_KV_HEADS * SB_DIM
    w_kpe = wi[:, o2:o3]
    wcat = jnp.concatenate([wi[:, :o2], wi[:, o3:], _pad_lanes(w_kpe), _pad_lanes(_rot_cols(w_kpe))], axis=1).astype(BF16)
    uq = w_uq[0].reshape(Q_RANK, MLA_HEADS, NOPE_DIM + ROPE_DIM)
    uq_pe = uq[:, :, NOPE_DIM:]
    wq = jnp.concatenate([uq[:, :, :NOPE_DIM].reshape(Q_RANK, -1),
                          _pad_lanes(uq_pe).reshape(Q_RANK, -1),
                          _pad_lanes(_rot_cols(uq_pe)).reshape(Q_RANK, -1)], axis=1).astype(BF16)
    wuk = w_uk[0].astype(BF16)
    wuv = w_uv[0].astype(BF16)
    wo = w_o[0].astype(BF16)
    n_mla = MLA_HEADS * V_DIM

    pos = jnp.concatenate([jnp.arange(s_len), jnp.full((bd,), past_len)]).astype(F32)
    inv = ROPE_THETA ** (-jnp.arange(half, dtype=F32) / half)
    ang = pos[:, None] * inv[None, :]
    cos = _pad_lanes(jnp.tile(jnp.cos(ang), (1, 2)))
    sin = _pad_lanes(jnp.tile(jnp.sin(ang), (1, 2)))

    x_all = jnp.concatenate([x_prompt.reshape(s_len, d), x_sample.reshape(bd, d)], axis=0)

    qf, kf, vv, latent, kpe, sbq, sbk, sbv, sbkb, sbvb = _project(
        x_all, row2(attn_norm[0]), wcat, row2(q_norm[0]), wq, row2(kv_norm[0]),
        wuk.reshape(KV_RANK, -1), wuv.reshape(KV_RANK, -1), cos, sin)
    kpe = kpe[:, :ROPE_DIM]

    mla_p = _mla_prompt(qf, kf, vv, s_len)
    sb_p = _sb_prompt(sbq, sbkb, sbvb, s_len)

    dec_block = s_len // bd
    qlat = _per_head_matmul(qf, jnp.transpose(wuk, (1, 0, 2)), dec_block, QK_PAD // LANES, nt=True)
    qf_s = qf[s_len:].reshape(bd, MLA_HEADS, QK_PAD)
    o_lat = _mla_decode(page_table, qlat.reshape(bd, MLA_HEADS, KV_RANK), qf_s[:, :, NOPE_DIM:NOPE_DIM + ROPE_DIM],
                        latent[s_len:].reshape(bd, 1, KV_RANK), kpe[s_len:].reshape(bd, 1, ROPE_DIM),
                        cache_mla_latent, cache_mla_krope)
    mla_s = _per_head_matmul(o_lat.reshape(bd, -1), jnp.transpose(wuv, (1, 0, 2)), 0, 1, nt=False)
    kvw = SB_KV_HEADS * SB_DIM
    sb_s = _sb_decode(page_table, sbq[s_len:].reshape(bd, SB_HEADS, SB_DIM),
                      cache_sb_k.reshape(*cache_sb_k.shape[:3], kvw), cache_sb_v.reshape(*cache_sb_v.shape[:3], kvw))

    mla_all = jnp.concatenate([mla_p, mla_s], axis=0)
    sb_all = jnp.concatenate([sb_p, sb_s.reshape(bd, -1)], axis=0)
    h1, qm = _out_proj(x_all, mla_all, sb_all, wo[:n_mla], wo[n_mla:], row2(xattn_norm[0]), w_mq[0].astype(BF16))

    n_mem = mem_prompt.shape[1]
    mw = MEM_HEADS * MEM_DIM
    mk, mv = _mem_kv(mem_prompt.reshape(n_mem, d), row2(mem_norm[0]), w_mk[0].astype(BF16), w_mv[0].astype(BF16))
    om_p = _mem_prompt(qm, mk, mv, s_len)
    om_s = _mem_decode(qm[s_len:].reshape(bd, 1, mw), cache_mem_k[0].reshape(bd, n_mem, mw), cache_mem_v[0].reshape(bd, n_mem, mw))
    om_all = jnp.concatenate([om_p, om_s.reshape(bd, mw)], axis=0)

    h2, hn, idx, gates, rank, counts = _route(h1, om_all, w_mo[0].astype(BF16), row2(moe_norm[0]),
                                              w_router[0], row2(b_router[0]))
    n_slots = (m_tot * TOP_K + N_EXPERTS * (MOE_CHUNK - 1)) // MOE_CHUNK * MOE_CHUNK
    n_sb = N_EXPERTS + n_slots // MOE_SUPER
    pad_start, sb_e, sb_start, sb_nch = _super_blocks(counts[0].astype(I32), n_sb)
    dest = (pad_start[idx] + rank).reshape(-1)
    xs = _dispatch(dest, hn, n_slots)
    yb = _experts(sb_e, sb_start, sb_nch, xs, w_gate[0], w_up[0], w_down[0],
                  b_gate[0][:, None, :], b_up[0][:, None, :], b_down[0][:, None, :])
    y_all = _combine(dest, gates, h2, row2(final_norm), yb)

    sp = slice(0, s_len)
    ss = slice(s_len, m_tot)
    kv_heads = (SB_KV_HEADS, SB_DIM)
    return (y_all[sp].reshape(1, s_len, d), y_all[ss].reshape(bd, 1, d),
            latent[sp].reshape(1, 1, s_len, KV_RANK), kpe[sp].reshape(1, 1, s_len, ROPE_DIM),
            sbk[sp].reshape(1, 1, s_len, *kv_heads), sbv[sp].reshape(1, 1, s_len, *kv_heads),
            mk.reshape(1, 1, n_mem, MEM_HEADS, MEM_DIM), mv.reshape(1, 1, n_mem, MEM_HEADS, MEM_DIM),
            latent[ss].reshape(1, bd, 1, KV_RANK), kpe[ss].reshape(1, bd, 1, ROPE_DIM),
            sbk[ss].reshape(1, bd, 1, *kv_heads), sbv[ss].reshape(1, bd, 1, *kv_heads))
```

```python
import functools

import jax
import jax.numpy as jnp
from jax import lax
from jax.experimental import pallas as pl
from jax.experimental.pallas import tpu as pltpu

F32 = jnp.float32
BF16 = jnp.bfloat16
I32 = jnp.int32
U32 = jnp.uint32

MLA_HEADS = 8
Q_RANK = 512
KV_RANK = 512
NOPE_DIM = 128
ROPE_DIM = 64
V_DIM = 128
ROPE_THETA = 10000.0
SB_HEADS = 8
SB_KV_HEADS = 2
SB_GROUP = SB_HEADS // SB_KV_HEADS
SB_DIM = 128
MEM_HEADS = 4
MEM_DIM = 128
N_EXPERTS = 32
TOP_K = 4
SWIGLU_LIMIT = 7.0
SWIGLU_ALPHA = 1.702
EPS = 1e-6
NEG_INF = -1e30
MLA_SCALE = (NOPE_DIM + ROPE_DIM) ** -0.5
SB_SCALE = SB_DIM ** -0.5
MEM_SCALE = MEM_DIM ** -0.5

LANES = 128
QK_PAD = 2 * LANES
VMEM_LIMIT = 56 * 1024 * 1024

SB_SKIP_LOG = -120.0

ROW_TILE = 320
MLA_TILE = 512
ATT_TILE = 256
MOE_TOK_TILE = 128
MOE_CHUNK = 256
MOE_SUPER = 2048
MOE_FF_TILE = 256
MLA_DEC_PAGES = 16
MLA_DEC_CHAINS = 2
SB_DEC_PAGES = 2

_NT = (((1,), (1,)), ((), ()))


def _rms(x, g):
    return x * lax.rsqrt(jnp.mean(x * x, axis=-1, keepdims=True) + EPS) * g


def _dot(a, b):
    return jnp.dot(a, b, preferred_element_type=F32)


def _dot_nt(a, b):
    return lax.dot_general(a, b, _NT, preferred_element_type=F32)


def _pack_bf16_pairs(x):
    bits = lax.bitcast_convert_type(x.astype(BF16).astype(F32), U32)
    half = x.shape[1] // 2
    return (bits[:, :half] >> 16) | (bits[:, half:] & jnp.uint32(0xFFFF0000))


def _unpack_bf16_pairs(w):
    lo = lax.bitcast_convert_type(w << 16, F32)
    hi = lax.bitcast_convert_type(w & jnp.uint32(0xFFFF0000), F32)
    return jnp.concatenate([lo, hi], axis=1).astype(BF16)


def _const_spec(shape):
    nd = len(shape)
    return pl.BlockSpec(shape, lambda *_: (0,) * nd)


def _params(sem, vmem=VMEM_LIMIT):
    return pltpu.CompilerParams(dimension_semantics=sem, vmem_limit_bytes=vmem)


def _proj_body(x_ref, ga_ref, wcat_ref, qn_ref, wq_ref, kvn_ref, wuk_ref, wuv_ref, cos_ref, sin_ref,
               qf_ref, kf_ref, v_ref, lat_ref, kpe_ref, sbq_ref, sbk_ref, sbv_ref, sbkb_ref, sbvb_ref):
    xn = _rms(x_ref[...], ga_ref[...]).astype(BF16)
    p = _dot(xn, wcat_ref[...])
    cos = cos_ref[...]
    sin = sin_ref[...]
    o_cq, o_ckv, o_sq, o_sk, o_sv, o_kp, o_kr = 0, 512, 1024, 2048, 2304, 2560, 2688

    cqn = _rms(p[:, o_cq:o_cq + Q_RANK], qn_ref[...]).astype(BF16)
    q = _dot(cqn, wq_ref[...])
    hw = MLA_HEADS * LANES
    for h in range(MLA_HEADS):
        sl = slice(h * LANES, (h + 1) * LANES)
        q_pe = q[:, hw + h * LANES: hw + (h + 1) * LANES] * cos + q[:, 2 * hw + h * LANES: 2 * hw + (h + 1) * LANES] * sin
        qf_ref[:, h * QK_PAD: h * QK_PAD + LANES] = (q[:, sl] * MLA_SCALE).astype(BF16)
        qf_ref[:, h * QK_PAD + LANES: (h + 1) * QK_PAD] = (q_pe * MLA_SCALE).astype(BF16)

    lat = _rms(p[:, o_ckv:o_ckv + KV_RANK], kvn_ref[...])
    lat_ref[...] = lat
    latb = lat.astype(BF16)
    kn = _dot(latb, wuk_ref[...])
    v_ref[...] = _dot(latb, wuv_ref[...]).astype(BF16)
    kpe = p[:, o_kp:o_kp + LANES] * cos + p[:, o_kr:o_kr + LANES] * sin
    kpe_ref[...] = kpe
    kpeb = kpe.astype(BF16)
    for h in range(MLA_HEADS):
        kf_ref[:, h * QK_PAD: h * QK_PAD + LANES] = kn[:, h * LANES:(h + 1) * LANES].astype(BF16)
        kf_ref[:, h * QK_PAD + LANES: (h + 1) * QK_PAD] = kpeb

    sbq_ref[...] = (p[:, o_sq:o_sq + SB_HEADS * SB_DIM] * SB_SCALE).astype(BF16)
    sbk = p[:, o_sk:o_sk + SB_KV_HEADS * SB_DIM]
    sbv = p[:, o_sv:o_sv + SB_KV_HEADS * SB_DIM]
    sbk_ref[...] = sbk
    sbv_ref[...] = sbv
    sbkb_ref[...] = sbk.astype(BF16)
    sbvb_ref[...] = sbv.astype(BF16)


def _project(x, ga, wcat, qn, wq, kvn, wuk, wuv, cos, sin):
    m, d = x.shape
    tm = ROW_TILE
    assert m % tm == 0
    row = lambda w: pl.BlockSpec((tm, w), lambda i: (i, 0))
    kvw = SB_KV_HEADS * SB_DIM
    outs = [((m, MLA_HEADS * QK_PAD), BF16), ((m, MLA_HEADS * QK_PAD), BF16), ((m, MLA_HEADS * V_DIM), BF16),
            ((m, KV_RANK), F32), ((m, LANES), F32), ((m, SB_HEADS * SB_DIM), BF16),
            ((m, kvw), F32), ((m, kvw), F32), ((m, kvw), BF16), ((m, kvw), BF16)]
    consts = [ga, wcat, qn, wq, kvn, wuk, wuv]
    return pl.pallas_call(
        _proj_body,
        grid=(m // tm,),
        in_specs=[row(d)] + [_const_spec(c.shape) for c in consts] + [row(LANES), row(LANES)],
        out_specs=[row(s[1]) for s, _ in outs],
        out_shape=[jax.ShapeDtypeStruct(s, dt) for s, dt in outs],
        name="proj_in",
        compiler_params=_params(("arbitrary",)),
    )(x, *consts, cos, sin)


def _mla_prompt_body(q_ref, k_ref, v_ref, o_ref, *, t):
    qi = pl.program_id(1)
    q = q_ref[...]

    def block(j, carry, masked):
        m, l, acc = carry
        start = pl.multiple_of(j * t, t)
        s = _dot_nt(q, k_ref[pl.ds(start, t), :])
        if masked:
            row = lax.broadcasted_iota(I32, (t, t), 0)
            col = lax.broadcasted_iota(I32, (t, t), 1)
            s = jnp.where(col <= row, s, NEG_INF)
        m_new = jnp.maximum(m, jnp.max(s, axis=-1, keepdims=True))
        alpha = jnp.exp(m - m_new)
        p = jnp.exp(s - m_new)
        l = alpha * l + jnp.sum(p, axis=-1, keepdims=True)
        acc = alpha * acc + _dot(p.astype(BF16), v_ref[pl.ds(start, t), :])
        return m_new, l, acc

    init = (jnp.full((t, 1), NEG_INF, F32), jnp.zeros((t, 1), F32), jnp.zeros((t, V_DIM), F32))
    carry = lax.fori_loop(0, qi, lambda j, c: block(j, c, False), init)
    _, l, acc = block(qi, carry, True)
    o_ref[...] = (acc / l).astype(BF16)


def _mla_prompt(qf, kf, vv, s_len):
    t = MLA_TILE
    assert s_len % t == 0
    return pl.pallas_call(
        functools.partial(_mla_prompt_body, t=t),
        grid=(MLA_HEADS, s_len // t),
        in_specs=[pl.BlockSpec((t, QK_PAD), lambda h, i: (i, h)),
                  pl.BlockSpec((s_len, QK_PAD), lambda h, i: (0, h)),
                  pl.BlockSpec((s_len, V_DIM), lambda h, i: (0, h))],
        out_specs=pl.BlockSpec((t, V_DIM), lambda h, i: (i, h)),
        out_shape=jax.ShapeDtypeStruct((s_len, MLA_HEADS * V_DIM), BF16),
        name="mla_prompt",
        compiler_params=_params(("arbitrary", "arbitrary")),
    )(qf, kf, vv)


def _sb_logs(z):
    t = jnp.log1p(jnp.exp(-jnp.abs(z)))
    return jnp.minimum(z, 0.0) - t, -jnp.maximum(z, 0.0) - t


def _later_mask(n):
    return (lax.broadcasted_iota(I32, (n, n), 0) > lax.broadcasted_iota(I32, (n, n), 1)).astype(BF16)


def _suffix_sums(x, u):
    hi = x.astype(BF16)
    lo = (x - hi.astype(F32)).astype(BF16)
    return _dot(hi, u) + _dot(lo, u)


def _sb_prompt_body(q_ref, k_ref, v_ref, o_ref, *, t):
    qi = pl.program_id(1)
    rows = SB_GROUP * t
    q = jnp.concatenate([q_ref[:, g * SB_DIM:(g + 1) * SB_DIM] for g in range(SB_GROUP)], axis=0)
    u = _later_mask(t)

    def block(j, later, acc, masked):
        start = pl.multiple_of(j * t, t)
        z = _dot_nt(q, k_ref[pl.ds(start, t), :])
        log_b, log_1m = _sb_logs(z)
        if masked:
            row = lax.broadcasted_iota(I32, (rows, t), 0) & (t - 1)
            col = lax.broadcasted_iota(I32, (rows, t), 1)
            valid = col < row
            log_1m = jnp.where(valid, log_1m, 0.0)
        w = jnp.exp(log_b + later + _suffix_sums(log_1m, u))
        if masked:
            w = jnp.where(valid, w, 0.0)
        acc = acc + _dot(w.astype(BF16), v_ref[pl.ds(start, t), :])
        return later + jnp.sum(log_1m, axis=-1, keepdims=True), acc

    later, acc = block(qi, jnp.zeros((rows, 1), F32), jnp.zeros((rows, SB_DIM), F32), True)

    def cond(c):
        j, later, _ = c
        return jnp.logical_and(j >= 0, jnp.max(later) > SB_SKIP_LOG)

    def body(c):
        j, later, acc = c
        later, acc = block(j, later, acc, False)
        return j - 1, later, acc

    _, _, acc = lax.while_loop(cond, body, (qi - 1, later, acc))
    for g in range(SB_GROUP):
        o_ref[:, g * SB_DIM:(g + 1) * SB_DIM] = acc[g * t:(g + 1) * t].astype(BF16)


def _sb_prompt(sbq, sbkb, sbvb, s_len):
    t = ATT_TILE
    gw = SB_GROUP * SB_DIM
    return pl.pallas_call(
        functools.partial(_sb_prompt_body, t=t),
        grid=(SB_KV_HEADS, s_len // t),
        in_specs=[pl.BlockSpec((t, gw), lambda c, i: (i, c)),
                  pl.BlockSpec((s_len, SB_DIM), lambda c, i: (0, c)),
                  pl.BlockSpec((s_len, SB_DIM), lambda c, i: (0, c))],
        out_specs=pl.BlockSpec((t, gw), lambda c, i: (i, c)),
        out_shape=jax.ShapeDtypeStruct((s_len, SB_HEADS * SB_DIM), BF16),
        name="sb_prompt",
        compiler_params=_params(("arbitrary", "arbitrary")),
    )(sbq, sbkb, sbvb)


def _head_matmul_body(a_ref, w_ref, o_ref, *, nt):
    w = w_ref[...]
    o_ref[...] = (_dot_nt(a_ref[...], w) if nt else _dot(a_ref[...], w)).astype(o_ref.dtype)


def _per_head_matmul(a, w, row_block, col_stride, nt):
    nh, wk, wn = w.shape
    a_width, out_w = (wn, wk) if nt else (wk, wn)
    bd = MOE_TOK_TILE
    return pl.pallas_call(
        functools.partial(_head_matmul_body, nt=nt),
        grid=(nh,),
        in_specs=[pl.BlockSpec((bd, a_width), lambda h: (row_block, h * col_stride)),
                  pl.BlockSpec((None, wk, wn), lambda h: (h, 0, 0))],
        out_specs=pl.BlockSpec((bd, out_w), lambda h: (0, h)),
        out_shape=jax.ShapeDtypeStruct((bd, nh * out_w), BF16),
        name="per_head_matmul",
        compiler_params=_params(("arbitrary",)),
    )(a, w)


def _mla_decode_body(pt_ref, ql_ref, qp_ref, ln_ref, kn_ref, lat_hbm, krt_hbm, o_ref, lat_buf, krt_buf, sem,
                     *, pp, chains, n_pages):
    b = pl.program_id(0)
    n_groups = n_pages // pp
    total = pl.num_programs(0) * n_groups
    first_t = b * n_groups
    ql = ql_ref[0]
    qp = qp_ref[0]

    def copies(t, slot):
        seq = lax.div(t, n_groups)
        first = lax.rem(t, n_groups) * pp
        out = []
        for i in range(pp):
            phys = pt_ref[seq, first + i]
            out.append(pltpu.make_async_copy(lat_hbm.at[0, phys], lat_buf.at[slot, i], sem.at[slot, 0]))
            out.append(pltpu.make_async_copy(krt_hbm.at[0, phys], krt_buf.at[slot, i], sem.at[slot, 1]))
        return out

    @pl.when(b == 0)
    def _():
        for cp in copies(0, 0):
            cp.start()

    ln = ln_ref[0]
    s_new = (jnp.sum(ql.astype(F32) * ln, axis=-1, keepdims=True)
             + jnp.sum(qp.astype(F32) * kn_ref[0], axis=-1, keepdims=True))
    empty = (jnp.full_like(s_new, NEG_INF), jnp.zeros_like(s_new), jnp.zeros((MLA_HEADS, KV_RANK), F32))
    init = ((s_new, jnp.ones_like(s_new), jnp.broadcast_to(ln, (MLA_HEADS, KV_RANK))),) + (empty,) * (chains - 1)
    per = pp // chains

    def step(g, carry):
        t = first_t + g
        slot = lax.rem(t, 2)
        for cp in copies(jnp.minimum(t + 1, total - 1), 1 - slot):
            cp.start()
        for cp in copies(t, slot):
            cp.wait()
        out = []
        for c in range(chains):
            m_prev, l_prev, acc_prev = carry[c]
            pages = range(c * per, (c + 1) * per)
            lat = jnp.concatenate([lat_buf[slot, i].astype(BF16) for i in pages], axis=0)
            krt = jnp.concatenate([krt_buf[slot, i].astype(BF16) for i in pages], axis=1)
            s = _dot_nt(ql, lat) + _dot(qp, krt)
            m_new = jnp.maximum(m_prev, jnp.max(s, axis=-1, keepdims=True))
            alpha = jnp.exp(m_prev - m_new)
            pr = jnp.exp(s - m_new)
            out.append((m_new, alpha * l_prev + jnp.sum(pr, axis=-1, keepdims=True),
                        alpha * acc_prev + _dot(pr.astype(BF16), lat)))
        return tuple(out)

    state = lax.fori_loop(0, n_groups, step, init)

    m = state[0][0]
    for c in range(1, chains):
        m = jnp.maximum(m, state[c][0])
    l = jnp.zeros_like(m)
    acc = jnp.zeros((MLA_HEADS, KV_RANK), F32)
    for c in range(chains):
        w = jnp.exp(state[c][0] - m)
        l = l + w * state[c][1]
        acc = acc + w * state[c][2]
    o_ref[0] = (acc / l).astype(BF16)

    @pl.when(b == pl.num_programs(0) - 1)
    def _():
        for cp in copies(total - 1, lax.rem(total, 2)):
            cp.wait()


def _mla_decode(page_table, qlat, qpe, lat_new, kpe_new, cache_lat, cache_krt):
    bd, n_pages = page_table.shape
    pp, chains = MLA_DEC_PAGES, MLA_DEC_CHAINS
    assert n_pages % pp == 0 and pp % chains == 0
    page = cache_lat.shape[2]
    per_seq = lambda w: pl.BlockSpec((1, MLA_HEADS, w), lambda b, pt: (b, 0, 0))
    new_tok = lambda w: pl.BlockSpec((1, 1, w), lambda b, pt: (b, 0, 0))
    any_spec = pl.BlockSpec(memory_space=pl.ANY)
    return pl.pallas_call(
        functools.partial(_mla_decode_body, pp=pp, chains=chains, n_pages=n_pages),
        grid_spec=pltpu.PrefetchScalarGridSpec(
            num_scalar_prefetch=1,
            grid=(bd,),
            in_specs=[per_seq(KV_RANK), per_seq(ROPE_DIM), new_tok(KV_RANK), new_tok(ROPE_DIM), any_spec, any_spec],
            out_specs=per_seq(KV_RANK),
            scratch_shapes=[pltpu.VMEM((2, pp, page, KV_RANK), F32), pltpu.VMEM((2, pp, ROPE_DIM, page), F32),
                            pltpu.SemaphoreType.DMA((2, 2))]),
        out_shape=jax.ShapeDtypeStruct((bd, MLA_HEADS, KV_RANK), BF16),
        name="mla_decode",
        compiler_params=_params(("arbitrary",)),
    )(page_table, qlat, qpe, lat_new, kpe_new, cache_lat, cache_krt)


def _sb_decode_body(pt_ref, q_ref, k_hbm, v_hbm, o_ref, kbuf, vbuf, sem, *, pp, page, n_pages):
    b = pl.program_id(0)
    n_groups = n_pages // pp
    q = q_ref[0]
    head_of_kv0 = lax.broadcasted_iota(I32, (SB_HEADS, 1), 0) < SB_GROUP
    u = _later_mask(page)

    def copies(seq, g, slot):
        out = []
        for i in range(pp):
            phys = pt_ref[seq, n_pages - (g + 1) * pp + i]
            out.append(pltpu.make_async_copy(k_hbm.at[0, phys], kbuf.at[slot, i], sem.at[slot, 0]))
            out.append(pltpu.make_async_copy(v_hbm.at[0, phys], vbuf.at[slot, i], sem.at[slot, 1]))
        return out

    def start(seq, g, slot):
        for cp in copies(seq, g, slot):
            cp.start()

    def wait(seq, g, slot):
        for cp in copies(seq, g, slot):
            cp.wait()

    def group(slot, later, acc):
        def per_kv_head(buf, c):
            return jnp.concatenate([buf[slot, i, pl.ds(c, page, stride=SB_KV_HEADS), :].astype(BF16)
                                    for i in range(pp)], axis=0)
        z = jnp.where(head_of_kv0, _dot_nt(q, per_kv_head(kbuf, 0)), _dot_nt(q, per_kv_head(kbuf, 1)))
        log_b, log_1m = _sb_logs(z)
        stacked = jnp.concatenate([log_1m[:, i * page:(i + 1) * page] for i in range(pp)], axis=0)
        inside = _suffix_sums(stacked, u)
        total = jnp.sum(stacked, axis=-1, keepdims=True)
        ws = [None] * pp
        for i in reversed(range(pp)):
            rs = slice(i * SB_HEADS, (i + 1) * SB_HEADS)
            ws[i] = jnp.exp(log_b[:, i * page:(i + 1) * page] + later + inside[rs])
            later = later + total[rs]
        w = jnp.concatenate(ws, axis=1).astype(BF16)
        acc = acc + jnp.where(head_of_kv0, _dot(w, per_kv_head(vbuf, 0)), _dot(w, per_kv_head(vbuf, 1)))
        return later, acc

    def unfinished(g, later):
        return jnp.logical_and(g < n_groups, jnp.max(later) > SB_SKIP_LOG)

    @pl.when(b == 0)
    def _():
        start(0, 0, 0)

    @pl.when(b + 1 < pl.num_programs(0))
    def _():
        start(b + 1, 0, (b + 1) % 2)

    wait(b, 0, b % 2)
    later, acc = group(b % 2, jnp.zeros((SB_HEADS, 1), F32), jnp.zeros((SB_HEADS, SB_DIM), F32))
    more = unfinished(1, later)

    @pl.when(more)
    def _():
        start(b, 1, 2)

    def body(c):
        g, _, later, acc = c
        slot = 2 + (g + 1) % 2
        wait(b, g, slot)
        later, acc = group(slot, later, acc)
        more = unfinished(g + 1, later)

        @pl.when(more)
        def _():
            start(b, g + 1, 2 + g % 2)
        return g + 1, more, later, acc

    _, _, _, acc = lax.while_loop(lambda c: c[1], body, (jnp.int32(1), more, later, acc))
    o_ref[0] = acc.astype(BF16)


def _sb_decode(page_table, q, cache_k, cache_v):
    bd, n_pages = page_table.shape
    pp = SB_DEC_PAGES
    assert n_pages % pp == 0
    rows = cache_k.shape[2]
    page = rows // SB_KV_HEADS
    per_seq = pl.BlockSpec((1, SB_HEADS, SB_DIM), lambda b, pt: (b, 0, 0))
    any_spec = pl.BlockSpec(memory_space=pl.ANY)
    return pl.pallas_call(
        functools.partial(_sb_decode_body, pp=pp, page=page, n_pages=n_pages),
        grid_spec=pltpu.PrefetchScalarGridSpec(
            num_scalar_prefetch=1,
            grid=(bd,),
            in_specs=[per_seq, any_spec, any_spec],
            out_specs=per_seq,
            scratch_shapes=[pltpu.VMEM((4, pp, rows, SB_DIM), F32), pltpu.VMEM((4, pp, rows, SB_DIM), F32),
                            pltpu.SemaphoreType.DMA((4, 2))]),
        out_shape=jax.ShapeDtypeStruct((bd, SB_HEADS, SB_DIM), BF16),
        name="sb_decode",
        compiler_params=_params(("arbitrary",)),
    )(page_table, q, cache_k, cache_v)


def _wo_body(x_ref, mla_ref, sb_ref, wo_a_ref, wo_b_ref, gx_ref, wmq_ref, h1_ref, qm_ref):
    h1 = x_ref[...] + _dot(mla_ref[...], wo_a_ref[...]) + _dot(sb_ref[...], wo_b_ref[...])
    h1_ref[...] = h1
    hn = _rms(h1, gx_ref[...]).astype(BF16)
    qm_ref[...] = (_dot(hn, wmq_ref[...]) * MEM_SCALE).astype(BF16)


def _out_proj(x, mla, sb, wo_a, wo_b, gx, wmq):
    m, d = x.shape
    tm = ROW_TILE
    row = lambda w: pl.BlockSpec((tm, w), lambda i: (i, 0))
    mw = MEM_HEADS * MEM_DIM
    return pl.pallas_call(
        _wo_body,
        grid=(m // tm,),
        in_specs=[row(d), row(mla.shape[1]), row(sb.shape[1])] + [_const_spec(c.shape) for c in (wo_a, wo_b, gx, wmq)],
        out_specs=[row(d), row(mw)],
        out_shape=[jax.ShapeDtypeStruct((m, d), F32), jax.ShapeDtypeStruct((m, mw), BF16)],
        name="out_proj",
        compiler_params=_params(("arbitrary",)),
    )(x, mla, sb, wo_a, wo_b, gx, wmq)


def _mem_kv_body(mem_ref, g_ref, wk_ref, wv_ref, k_ref, v_ref):
    m = _rms(mem_ref[...], g_ref[...]).astype(BF16)
    k_ref[...] = _dot(m, wk_ref[...])
    v_ref[...] = _dot(m, wv_ref[...])


def _mem_kv(mem, g, wk, wv):
    n = mem.shape[0]
    mw = MEM_HEADS * MEM_DIM
    args = (mem, g, wk, wv)
    return pl.pallas_call(
        _mem_kv_body,
        grid=(1,),
        in_specs=[_const_spec(a.shape) for a in args],
        out_specs=[_const_spec((n, mw))] * 2,
        out_shape=[jax.ShapeDtypeStruct((n, mw), F32)] * 2,
        name="mem_kv",
        compiler_params=_params(("arbitrary",)),
    )(*args)


def _mem_prompt_body(q_ref, k_ref, v_ref, o_ref):
    k = k_ref[...].astype(BF16)
    v = v_ref[...].astype(BF16)
    for h in range(MEM_HEADS):
        sl = slice(h * MEM_DIM, (h + 1) * MEM_DIM)
        s = _dot_nt(q_ref[:, sl], k[:, sl])
        e = jnp.exp(s - jnp.max(s, axis=-1, keepdims=True))
        p = e / jnp.sum(e, axis=-1, keepdims=True)
        o_ref[:, sl] = _dot(p.astype(BF16), v[:, sl]).astype(BF16)


def _mem_prompt(qm, mk, mv, s_len):
    tm = ATT_TILE
    mw = MEM_HEADS * MEM_DIM
    row = pl.BlockSpec((tm, mw), lambda i: (i, 0))
    return pl.pallas_call(
        _mem_prompt_body,
        grid=(s_len // tm,),
        in_specs=[row, _const_spec(mk.shape), _const_spec(mv.shape)],
        out_specs=row,
        out_shape=jax.ShapeDtypeStruct((s_len, mw), BF16),
        name="mem_prompt",
        compiler_params=_params(("arbitrary",)),
    )(qm, mk, mv)


def _mem_decode_body(q_ref, k_ref, v_ref, o_ref):
    n = k_ref.shape[1] // MEM_HEADS
    q = q_ref[0].astype(F32)
    for h in range(MEM_HEADS):
        sl = slice(h * MEM_DIM, (h + 1) * MEM_DIM)
        rows = pl.ds(h, n, stride=MEM_HEADS)
        s = jnp.sum(k_ref[0, rows, :] * q[:, sl], axis=-1, keepdims=True)
        e = jnp.exp(s - jnp.max(s, axis=0, keepdims=True))
        p = e / jnp.sum(e, axis=0, keepdims=True)
        o_ref[0, :, sl] = jnp.sum(p * v_ref[0, rows, :], axis=0, keepdims=True).astype(BF16)


def _mem_decode(qm, cache_k, cache_v):
    bd, _, mw = qm.shape
    n = cache_k.shape[1]
    one = pl.BlockSpec((1, 1, mw), lambda b: (b, 0, 0))
    kv = pl.BlockSpec((1, n, MEM_DIM), lambda b: (b, 0, 0))
    return pl.pallas_call(
        _mem_decode_body,
        grid=(bd,),
        in_specs=[one, kv, kv],
        out_specs=one,
        out_shape=jax.ShapeDtypeStruct((bd, 1, mw), BF16),
        name="mem_decode",
        compiler_params=_params(("arbitrary",)),
    )(qm, cache_k, cache_v)


def _route_body(h1_ref, o_ref, wmo_ref, gm_ref, wr_ref, br_ref,
                h2_ref, hn_ref, idx_ref, gate_ref, rank_ref, cnt_ref, run_sc):
    @pl.when(pl.program_id(0) == 0)
    def _():
        run_sc[...] = jnp.zeros_like(run_sc)

    h2 = h1_ref[...] + _dot(o_ref[...], wmo_ref[...])
    h2_ref[...] = h2
    hn = _rms(h2, gm_ref[...])
    hn_ref[...] = _pack_bf16_pairs(hn)
    logits = jnp.dot(hn, wr_ref[...], precision=lax.Precision.HIGHEST, preferred_element_type=F32) + br_ref[...]

    tm = logits.shape[0]
    lane = lax.broadcasted_iota(I32, (tm, N_EXPERTS), 1)
    work = logits
    vals, idxs = [], []
    for _ in range(TOP_K):
        mx = jnp.max(work, axis=-1, keepdims=True)
        ix = jnp.min(jnp.where(work == mx, lane, N_EXPERTS), axis=-1, keepdims=True)
        vals.append(mx)
        idxs.append(ix)
        work = jnp.where(lane == ix, -jnp.inf, work)
    es = [jnp.exp(v - vals[0]) for v in vals]
    den = es[0] + es[1] + es[2] + es[3]

    chosen = (lane == idxs[0]) | (lane == idxs[1]) | (lane == idxs[2]) | (lane == idxs[3])
    onehot = jnp.where(chosen, 1.0, 0.0)
    earlier = (lax.broadcasted_iota(I32, (tm, tm), 0) > lax.broadcasted_iota(I32, (tm, tm), 1)).astype(BF16)
    rank = run_sc[...] + _dot(earlier, onehot.astype(BF16))
    for k in range(TOP_K):
        idx_ref[:, k:k + 1] = idxs[k]
        gate_ref[:, k:k + 1] = es[k] / den
        rank_ref[:, k:k + 1] = jnp.sum(jnp.where(lane == idxs[k], rank, 0.0), axis=-1, keepdims=True).astype(I32)
    run_sc[...] = run_sc[...] + jnp.sum(onehot, axis=0, keepdims=True)
    cnt_ref[...] = run_sc[...]


def _route(h1, o, wmo, gm, wr, br):
    m, d = h1.shape
    tm = MOE_TOK_TILE
    row = lambda w: pl.BlockSpec((tm, w), lambda i: (i, 0))
    outs = [((m, d), F32), ((m, d // 2), U32), ((m, TOP_K), I32), ((m, TOP_K), F32), ((m, TOP_K), I32)]
    return pl.pallas_call(
        _route_body,
        grid=(m // tm,),
        in_specs=[row(d), row(o.shape[1])] + [_const_spec(c.shape) for c in (wmo, gm, wr, br)],
        out_specs=[row(s[1]) for s, _ in outs] + [_const_spec((1, N_EXPERTS))],
        out_shape=[jax.ShapeDtypeStruct(s, dt) for s, dt in outs] + [jax.ShapeDtypeStruct((1, N_EXPERTS), F32)],
        scratch_shapes=[pltpu.VMEM((1, N_EXPERTS), F32)],
        name="moe_route",
        compiler_params=_params(("arbitrary",)),
    )(h1, o, wmo, gm, wr, br)


def _dispatch_body(dest_ref, hn_ref, zero_ref, xs_ref, sem):
    del zero_ref
    tm = hn_ref.shape[0]

    def row_copy(t, d):
        return pltpu.make_async_copy(hn_ref.at[pl.ds(t, 1)], xs_ref.at[pl.ds(d, 1)], sem)

    def issue(t, c):
        for k in range(TOP_K):
            row_copy(t, dest_ref[t * TOP_K + k]).start()
        return c

    def drain(t, c):
        for k in range(TOP_K):
            row_copy(t, dest_ref[t * TOP_K + k]).wait()
        return c

    lax.fori_loop(0, tm, issue, 0)
    lax.fori_loop(0, tm, drain, 0)


def _dispatch(dest_flat, hn, n_slots):
    m, d = hn.shape
    tm = MOE_TOK_TILE
    zeros = jnp.zeros((n_slots, d), hn.dtype)
    return pl.pallas_call(
        _dispatch_body,
        grid=(m // tm,),
        in_specs=[pl.BlockSpec((tm * TOP_K,), lambda i: (i,), memory_space=pltpu.SMEM),
                  pl.BlockSpec((tm, d), lambda i: (i, 0)),
                  pl.BlockSpec(memory_space=pl.ANY)],
        out_specs=pl.BlockSpec(memory_space=pl.ANY),
        out_shape=jax.ShapeDtypeStruct((n_slots, d), hn.dtype),
        scratch_shapes=[pltpu.SemaphoreType.DMA(())],
        input_output_aliases={2: 0},
        name="moe_dispatch",
        compiler_params=_params(("arbitrary",)),
    )(dest_flat, hn, zeros)


def _expert_body(sb_e, sb_start, sb_nch, tail, xs_ref, wg_ref, wu_ref, wd_ref, bg_ref, bu_ref, bd_ref, yb_ref,
                 x_sc, y_sc, wg_sc, wu_sc, wd_sc, sem):
    del sb_e
    i = pl.program_id(0)
    j = pl.program_id(1)
    last_j = j == pl.num_programs(1) - 1
    nch = sb_nch[i]
    start = sb_start[i]
    ch = MOE_CHUNK

    def rows(c):
        return pl.ds(pl.multiple_of(c * ch, ch), ch)

    def slot_rows(first, c):
        return pl.ds(pl.multiple_of(first + c * ch, ch), ch)

    def for_chunks(n, fn):
        lax.fori_loop(0, n, lambda c, carry: (fn(c), carry)[1], 0)

    @pl.when(nch > 0)
    def _():
        @pl.when(j == 0)
        def _():
            load = lambda c: pltpu.make_async_copy(xs_ref.at[slot_rows(start, c)], x_sc.at[rows(c)], sem)
            for_chunks(nch, lambda c: load(c).start())

            def init(c):
                y_sc[rows(c), :] = jnp.broadcast_to(bd_ref[...], (ch, y_sc.shape[1]))
            for_chunks(nch, init)
            for_chunks(nch, lambda c: load(c).wait())

        wg_sc[...] = wg_ref[...].astype(BF16)
        wu_sc[...] = wu_ref[...].astype(BF16)
        wd_sc[...] = wd_ref[...].astype(BF16)
        bg = bg_ref[...]
        bu = bu_ref[...]

        def chunk(c):
            x = _unpack_bf16_pairs(x_sc[rows(c), :])
            g = jnp.minimum(_dot(x, wg_sc[...]) + bg, SWIGLU_LIMIT)
            u = jnp.clip(_dot(x, wu_sc[...]) + bu, -SWIGLU_LIMIT, SWIGLU_LIMIT)
            a = g * jax.nn.sigmoid(SWIGLU_ALPHA * g) * (u + 1.0)
            y_sc[rows(c), :] += _dot(a.astype(BF16), wd_sc[...])
        for_chunks(nch, chunk)

        @pl.when(last_j)
        def _():
            store = lambda c: pltpu.make_async_copy(y_sc.at[rows(c)], yb_ref.at[slot_rows(start, c)], sem)
            for_chunks(nch, lambda c: store(c).start())
            for_chunks(nch, lambda c: store(c).wait())

    @pl.when(jnp.logical_and(i == pl.num_programs(0) - 1, last_j))
    def _():
        first = tail[0]
        n_tail = (yb_ref.shape[0] - first) // ch
        y_sc[rows(0), :] = jnp.zeros((ch, y_sc.shape[1]), F32)
        fill = lambda c: pltpu.make_async_copy(y_sc.at[rows(0)], yb_ref.at[slot_rows(first, c)], sem)
        for_chunks(n_tail, lambda c: fill(c).start())
        for_chunks(n_tail, lambda c: fill(c).wait())


def _experts(sb_e, sb_start, sb_nch, tail, xs, wg, wu, wd, bg, bu, bd):
    n_slots = xs.shape[0]
    d, d_ff = wg.shape[1:]
    tf = MOE_FF_TILE
    nj = d_ff // tf
    n_sb = sb_e.shape[0]

    def jj(i, j, n):
        return jnp.where(n[i] > 0, j, nj - 1)

    return pl.pallas_call(
        _expert_body,
        grid_spec=pltpu.PrefetchScalarGridSpec(
            num_scalar_prefetch=4,
            grid=(n_sb, nj),
            in_specs=[pl.BlockSpec(memory_space=pl.ANY),
                      pl.BlockSpec((None, d, tf), lambda i, j, e, s, n, t: (e[i], 0, jj(i, j, n))),
                      pl.BlockSpec((None, d, tf), lambda i, j, e, s, n, t: (e[i], 0, jj(i, j, n))),
                      pl.BlockSpec((None, tf, d), lambda i, j, e, s, n, t: (e[i], jj(i, j, n), 0)),
                      pl.BlockSpec((None, 1, tf), lambda i, j, e, s, n, t: (e[i], 0, jj(i, j, n))),
                      pl.BlockSpec((None, 1, tf), lambda i, j, e, s, n, t: (e[i], 0, jj(i, j, n))),
                      pl.BlockSpec((None, 1, d), lambda i, j, e, s, n, t: (e[i], 0, 0))],
            out_specs=pl.BlockSpec(memory_space=pl.ANY),
            scratch_shapes=[pltpu.VMEM((MOE_SUPER, d // 2), U32), pltpu.VMEM((MOE_SUPER, d), F32),
                            pltpu.VMEM((d, tf), BF16), pltpu.VMEM((d, tf), BF16), pltpu.VMEM((tf, d), BF16),
                            pltpu.SemaphoreType.DMA(())]),
        out_shape=jax.ShapeDtypeStruct((n_slots, d), F32),
        name="moe_experts",
        compiler_params=_params(("arbitrary", "arbitrary")),
    )(sb_e, sb_start, sb_nch, tail, xs, wg, wu, wd, bg, bu, bd)


def _combine_body(dest_ref, gate_ref, h2_ref, gf_ref, yb_ref, y_ref, buf, sem):
    tm = h2_ref.shape[0]

    def row_copy(t, k):
        return pltpu.make_async_copy(yb_ref.at[pl.ds(dest_ref[t * TOP_K + k], 1)], buf.at[k, pl.ds(t, 1)], sem)

    def issue(t, c):
        for k in range(TOP_K):
            row_copy(t, k).start()
        return c

    def drain(t, c):
        for k in range(TOP_K):
            row_copy(t, k).wait()
        return c

    lax.fori_loop(0, tm, issue, 0)
    lax.fori_loop(0, tm, drain, 0)
    gate = gate_ref[...]
    moe = gate[:, 0:1] * buf[0]
    for k in range(1, TOP_K):
        moe = moe + gate[:, k:k + 1] * buf[k]
    y_ref[...] = _rms(h2_ref[...] + moe, gf_ref[...])


def _combine(dest_flat, gates, h2, gf, yb):
    m, d = h2.shape
    tm = MOE_TOK_TILE
    return pl.pallas_call(
        _combine_body,
        grid=(m // tm,),
        in_specs=[pl.BlockSpec((tm * TOP_K,), lambda i: (i,), memory_space=pltpu.SMEM),
                  pl.BlockSpec((tm, TOP_K), lambda i: (i, 0)),
                  pl.BlockSpec((tm, d), lambda i: (i, 0)),
                  _const_spec(gf.shape),
                  pl.BlockSpec(memory_space=pl.ANY)],
        out_specs=pl.BlockSpec((tm, d), lambda i: (i, 0)),
        out_shape=jax.ShapeDtypeStruct((m, d), F32),
        scratch_shapes=[pltpu.VMEM((TOP_K, tm, d), F32), pltpu.SemaphoreType.DMA(())],
        name="moe_combine",
        compiler_params=_params(("arbitrary",)),
    )(dest_flat, gates, h2, gf, yb)


def _rot_cols(w):
    half = w.shape[-1] // 2
    return jnp.concatenate([-w[..., half:], w[..., :half]], axis=-1)


def _pad_lanes(w):
    return jnp.pad(w, [(0, 0)] * (w.ndim - 1) + [(0, LANES - w.shape[-1])])


def _super_blocks(counts, n_sb):
    padded = (counts + MOE_CHUNK - 1) // MOE_CHUNK * MOE_CHUNK
    pad_end = jnp.cumsum(padded)
    pad_start = pad_end - padded
    per_e = (padded + MOE_SUPER - 1) // MOE_SUPER
    sb_end = jnp.cumsum(per_e)
    i = jnp.arange(n_sb, dtype=I32)
    live = i < sb_end[-1]
    e = jnp.sum(sb_end[None, :] <= jnp.minimum(i, sb_end[-1] - 1)[:, None], axis=1).astype(I32)
    e = jnp.minimum(e, N_EXPERTS - 1)
    local = i - (sb_end[e] - per_e[e])
    rows = jnp.clip(padded[e] - local * MOE_SUPER, 0, MOE_SUPER)
    nch = jnp.where(live, rows // MOE_CHUNK, 0).astype(I32)
    start = jnp.where(live, pad_start[e] + local * MOE_SUPER, 0).astype(I32)
    return pad_start.astype(I32), e, start, nch, pad_end[-1:].astype(I32)


def kernel(x_prompt, x_sample, mem_prompt, cache_mla_latent, cache_mla_krope, cache_sb_k, cache_sb_v, cache_mem_k, cache_mem_v, page_table, attn_norm, w_in, q_norm, w_uq, kv_norm, w_uk, w_uv, w_o, xattn_norm, mem_norm, w_mq, w_mk, w_mv, w_mo, moe_norm, w_router, b_router, w_gate, b_gate, w_up, b_up, w_down, b_down, final_norm):
    depth = w_in.shape[0]
    assert depth == 1, "single-layer step"
    batch, s_len, d = x_prompt.shape
    bd, t_dec, _ = x_sample.shape
    assert batch == 1 and t_dec == 1 and bd == MOE_TOK_TILE
    n_pages = page_table.shape[1]
    page = cache_mla_latent.shape[2]
    past_len = n_pages * page
    m_tot = s_len + bd
    half = ROPE_DIM // 2
    row2 = lambda a: a.reshape(1, -1)

    wi = w_in[0]
    o1, o2, o3 = Q_RANK, Q_RANK + KV_RANK, Q_RANK + KV_RANK + ROPE_DIM
    o4 = o3 + SB_HEADS * SB_DIM
    o5 = o4 + SB_KV_HEADS * SB_DIM
    w_kpe = wi[:, o2:o3]
    wcat = jnp.concatenate([wi[:, :o2], wi[:, o3:], _pad_lanes(w_kpe), _pad_lanes(_rot_cols(w_kpe))], axis=1).astype(BF16)
    uq = w_uq[0].reshape(Q_RANK, MLA_HEADS, NOPE_DIM + ROPE_DIM)
    uq_pe = uq[:, :, NOPE_DIM:]
    wq = jnp.concatenate([uq[:, :, :NOPE_DIM].reshape(Q_RANK, -1),
                          _pad_lanes(uq_pe).reshape(Q_RANK, -1),
                          _pad_lanes(_rot_cols(uq_pe)).reshape(Q_RANK, -1)], axis=1).astype(BF16)
    wuk = w_uk[0].astype(BF16)
    wuv = w_uv[0].astype(BF16)
    wo = w_o[0].astype(BF16)
    n_mla = MLA_HEADS * V_DIM

    pos = jnp.concatenate([jnp.arange(s_len), jnp.full((bd,), past_len)]).astype(F32)
    inv = ROPE_THETA ** (-jnp.arange(half, dtype=F32) / half)
    ang = pos[:, None] * inv[None, :]
    cos = _pad_lanes(jnp.tile(jnp.cos(ang), (1, 2)))
    sin = _pad_lanes(jnp.tile(jnp.sin(ang), (1, 2)))

    x_all = jnp.concatenate([x_prompt.reshape(s_len, d), x_sample.reshape(bd, d)], axis=0)

    qf, kf, vv, latent, kpe, sbq, sbk, sbv, sbkb, sbvb = _project(
        x_all, row2(attn_norm[0]), wcat, row2(q_norm[0]), wq, row2(kv_norm[0]),
        wuk.reshape(KV_RANK, -1), wuv.reshape(KV_RANK, -1), cos, sin)
    kpe = kpe[:, :ROPE_DIM]

    mla_p = _mla_prompt(qf, kf, vv, s_len)
    sb_p = _sb_prompt(sbq, sbkb, sbvb, s_len)

    dec_block = s_len // bd
    qlat = _per_head_matmul(qf, jnp.transpose(wuk, (1, 0, 2)), dec_block, QK_PAD // LANES, nt=True)
    qf_s = qf[s_len:].reshape(bd, MLA_HEADS, QK_PAD)
    o_lat = _mla_decode(page_table, qlat.reshape(bd, MLA_HEADS, KV_RANK), qf_s[:, :, NOPE_DIM:NOPE_DIM + ROPE_DIM],
                        latent[s_len:].reshape(bd, 1, KV_RANK), kpe[s_len:].reshape(bd, 1, ROPE_DIM),
                        cache_mla_latent, jnp.swapaxes(cache_mla_krope, 2, 3))
    mla_s = _per_head_matmul(o_lat.reshape(bd, -1), jnp.transpose(wuv, (1, 0, 2)), 0, 1, nt=False)
    sb_view = lambda c: c.reshape(c.shape[0], c.shape[1], page * SB_KV_HEADS, SB_DIM)
    sb_s = _sb_decode(page_table, sbq[s_len:].reshape(bd, SB_HEADS, SB_DIM), sb_view(cache_sb_k), sb_view(cache_sb_v))

    mla_all = jnp.concatenate([mla_p, mla_s], axis=0)
    sb_all = jnp.concatenate([sb_p, sb_s.reshape(bd, -1)], axis=0)
    h1, qm = _out_proj(x_all, mla_all, sb_all, wo[:n_mla], wo[n_mla:], row2(xattn_norm[0]), w_mq[0].astype(BF16))

    n_mem = mem_prompt.shape[1]
    mw = MEM_HEADS * MEM_DIM
    mk, mv = _mem_kv(mem_prompt.reshape(n_mem, d), row2(mem_norm[0]), w_mk[0].astype(BF16), w_mv[0].astype(BF16))
    om_p = _mem_prompt(qm, mk, mv, s_len)
    mem_view = lambda c: c[0].reshape(bd, n_mem * MEM_HEADS, MEM_DIM)
    om_s = _mem_decode(qm[s_len:].reshape(bd, 1, mw), mem_view(cache_mem_k), mem_view(cache_mem_v))
    om_all = jnp.concatenate([om_p, om_s.reshape(bd, mw)], axis=0)

    h2, hn, idx, gates, rank, counts = _route(h1, om_all, w_mo[0].astype(BF16), row2(moe_norm[0]),
                                              w_router[0], row2(b_router[0]))
    n_slots = (m_tot * TOP_K + N_EXPERTS * (MOE_CHUNK - 1)) // MOE_CHUNK * MOE_CHUNK
    n_sb = N_EXPERTS + n_slots // MOE_SUPER
    pad_start, sb_e, sb_start, sb_nch, tail = _super_blocks(counts[0].astype(I32), n_sb)
    dest = (pad_start[idx] + rank).reshape(-1)
    xs = _dispatch(dest, hn, n_slots)
    yb = _experts(sb_e, sb_start, sb_nch, tail, xs, w_gate[0], w_up[0], w_down[0],
                  b_gate[0][:, None, :], b_up[0][:, None, :], b_down[0][:, None, :])
    y_all = _combine(dest, gates, h2, row2(final_norm), yb)

    sp = slice(0, s_len)
    ss = slice(s_len, m_tot)
    kv_heads = (SB_KV_HEADS, SB_DIM)
    return (y_all[sp].reshape(1, s_len, d), y_all[ss].reshape(bd, 1, d),
            latent[sp].reshape(1, 1, s_len, KV_RANK), kpe[sp].reshape(1, 1, s_len, ROPE_DIM),
            sbk[sp].reshape(1, 1, s_len, *kv_heads), sbv[sp].reshape(1, 1, s_len, *kv_heads),
            mk.reshape(1, 1, n_mem, MEM_HEADS, MEM_DIM), mv.reshape(1, 1, n_mem, MEM_HEADS, MEM_DIM),
            latent[ss].reshape(1, bd, 1, KV_RANK), kpe[ss].reshape(1, bd, 1, ROPE_DIM),
            sbk[ss].reshape(1, bd, 1, *kv_heads), sbv[ss].reshape(1, bd, 1, *kv_heads))
```
